```python
import math
import jax, jax.numpy as jnp
from jax import lax
import numpy as np

D_MODEL = 2048
BATCH = 32
SEQ = 256
DEPTH = 2
DEC_BATCH = 8
DEC_SEQ = 4096
PAST_LEN = 512

GRID_W = 64
N_BRANCH = 4
D_MIX = D_MODEL // 4
V_DIM = 128
H_A = D_MIX // V_DIM
DH = V_DIM // 2
ROPE_PAIRS = DH // 4
ROPE_BASE = 10000.0
Q_BLOCK = 128
CONV_B = 31
CONV_C = 3
CONV_R = 4
BS_R = 64
NB_R = D_MIX // BS_R
LRU_C = 8.0
D_FF = ((8 * D_MODEL // 3 + 127) // 128) * 128
CONV_FF = 3
EPS = 1e-6
IN_SIZES = (D_MIX, D_MIX, D_MIX, 2 * D_MIX, 3 * D_MIX, 2 * D_MIX, N_BRANCH * D_MODEL)
IN_SPLITS = tuple(int(s) for s in np.cumsum(IN_SIZES)[:-1])
N_IN = int(sum(IN_SIZES))

kernel_name = 'hybrid_diff_flow_trunk_step'


def rms_norm(x, g):
    xf = x.astype(jnp.float32)
    y = xf * lax.rsqrt(jnp.mean(xf * xf, axis=-1, keepdims=True) + EPS)
    return (y * g.astype(jnp.float32)).astype(x.dtype)


def layer_norm(x, g, b):
    xf = x.astype(jnp.float32)
    mu = jnp.mean(xf, axis=-1, keepdims=True)
    xc = xf - mu
    y = xc * lax.rsqrt(jnp.mean(xc * xc, axis=-1, keepdims=True) + EPS)
    return (y * g.astype(jnp.float32) + b.astype(jnp.float32)).astype(x.dtype)


def dwconv(x, w):
    k = w.shape[0]
    return lax.conv_general_dilated(
        x, w[:, None, :].astype(x.dtype), window_strides=(1,),
        padding=[((k - 1) // 2, k // 2)],
        dimension_numbers=('NWC', 'WIO', 'NWC'),
        feature_group_count=x.shape[-1])


def grid_angles(n):
    rows = n // GRID_W
    t_row = jnp.repeat(jnp.arange(rows), GRID_W).astype(jnp.float32)
    t_col = jnp.tile(jnp.arange(GRID_W), rows).astype(jnp.float32)
    inv = jnp.power(ROPE_BASE, -jnp.arange(ROPE_PAIRS, dtype=jnp.float32) / ROPE_PAIRS)
    return t_row[:, None] * inv, t_col[:, None] * inv


def _rot(xh, ang):
    cos = jnp.cos(ang)[:, None, None, :]
    sin = jnp.sin(ang)[:, None, None, :]
    x1, x2 = jnp.split(xh, 2, axis=-1)
    return jnp.concatenate([x1 * cos - x2 * sin, x1 * sin + x2 * cos], axis=-1)


def apply_rope_2d(x, angles):
    ang_row, ang_col = angles
    xf = x.astype(jnp.float32)
    xr, xc = jnp.split(xf, 2, axis=-1)
    return jnp.concatenate([_rot(xr, ang_row), _rot(xc, ang_col)], axis=-1).astype(x.dtype)


def diff_attention(q, k, v, lam):
    b, n = q.shape[0], q.shape[1]
    nb = n // Q_BLOCK
    qb = q.reshape(b, nb, Q_BLOCK, H_A, 2, DH).swapaxes(0, 1)
    scale = DH ** -0.5

    def block(qi):
        s = jnp.einsum('bqhjd,bkhjd->bhjqk', qi, k).astype(jnp.float32) * scale
        p = jax.nn.softmax(s, axis=-1)
        w = p[:, :, 0] - lam * p[:, :, 1]
        return jnp.einsum('bhqk,bkhe->bqhe', w.astype(v.dtype), v)

    o = lax.map(block, qb)
    return o.swapaxes(0, 1).reshape(b, n, H_A, V_DIM)


def _lin_combine(e1, e2):
    a1, b1 = e1
    a2, b2 = e2
    return a1 * a2, a2 * b1 + b2


def lru_scan(a, u, h0, reverse):
    if h0 is not None:
        idx = -1 if reverse else 0
        u = u.at[:, idx].add(a[:, idx] * h0)
    _, h = lax.associative_scan(_lin_combine, (a, u), axis=1, reverse=reverse)
    return h


def token_mixers(h, l, P, angles, ctx_k, ctx_v, h0):
    b, n, _ = h.shape
    f32 = jnp.float32
    proj = h @ P['w_in'][l]
    q, k, v, zb, zc, zr, zg = jnp.split(proj, IN_SPLITS, axis=-1)

    q = q.reshape(b, n, H_A, 2, DH)
    k = k.reshape(b, n, H_A, 2, DH)
    v = v.reshape(b, n, H_A, V_DIM)
    if angles is not None:
        q = apply_rope_2d(q, angles)
        k = apply_rope_2d(k, angles)
    if ctx_k is None:
        k_all, v_all = k, v
    else:
        k_all = jnp.concatenate([ctx_k.astype(k.dtype), k], axis=1)
        v_all = jnp.concatenate([ctx_v.astype(v.dtype), v], axis=1)
    lam_init = 0.8 - 0.6 * math.exp(-0.3 * l)
    lam = (jnp.exp(jnp.sum(P['lam_q1'][l].astype(f32) * P['lam_k1'][l].astype(f32)))
           - jnp.exp(jnp.sum(P['lam_q2'][l].astype(f32) * P['lam_k2'][l].astype(f32))) + lam_init)
    o = diff_attention(q, k_all, v_all, lam)
    y_a = (rms_norm(o, P['g_subln'][l]) * (1.0 - lam_init)).reshape(b, n, D_MIX)

    za, zgl = jnp.split(zb, 2, axis=-1)
    ub = za * jax.nn.sigmoid(zgl)
    ub = dwconv(ub, P['w_dw31'][l]) + P['b_dw31'][l]
    y_b = jax.nn.silu(layer_norm(ub, P['g_ln_conv'][l], P['b_ln_conv'][l]))

    gb, gc, xc = jnp.split(zc, 3, axis=-1)
    y_c = gb * dwconv(gc * xc, P['w_dw3'][l])

    xr_in, yr = jnp.split(zr, 2, axis=-1)
    xr = dwconv(xr_in, P['w_conv4'][l]) + P['b_conv4'][l]
    xrf = xr.astype(f32)
    xb = xr.reshape(b, n, NB_R, BS_R)

    def rglru(d, reverse):
        r = jax.nn.sigmoid((jnp.einsum('bnhi,hij->bnhj', xb, P['w_rg_a'][l, d]).reshape(b, n, D_MIX)
                            + P['b_rg_a'][l, d]).astype(f32))
        i = jax.nn.sigmoid((jnp.einsum('bnhi,hij->bnhj', xb, P['w_rg_x'][l, d]).reshape(b, n, D_MIX)
                            + P['b_rg_x'][l, d]).astype(f32))
        log_a = -LRU_C * r * jax.nn.softplus(-P['lru_lambda'][l, d].astype(f32))
        a = jnp.exp(log_a)
        u = jnp.sqrt(jnp.maximum(1.0 - a * a, 0.0)) * (i * xrf)
        h_init = None if h0 is None else h0[:, d].astype(f32)
        return lru_scan(a, u, h_init, reverse)

    h_f = rglru(0, False)
    h_b = rglru(1, True)
    y_d = ((h_f + h_b) * jax.nn.gelu(yr.astype(f32))).astype(h.dtype)
    h_last = jnp.stack([h_f[:, -1], h_b[:, 0]], axis=1).astype(h.dtype)

    merged = None
    for j, y_j in enumerate((y_a, y_b, y_c, y_d)):
        g_j = jax.nn.sigmoid(zg[..., j * D_MODEL:(j + 1) * D_MODEL])
        term = g_j * (y_j @ P['w_branch'][l, j])
        merged = term if merged is None else merged + term
    return merged @ P['w_out'][l], k, v, h_last


def conv_ffn(h, l, P):
    a, u = jnp.split(h @ P['w_ffn_up'][l], 2, axis=-1)
    a = dwconv(a, P['w_ffn_conv'][l]) + P['b_ffn_conv'][l]
    return (jax.nn.silu(a) * u) @ P['w_ffn_down'][l]


def trunk_layer(x, cvec, l, P, angles, ctx_k, ctx_v, h0):
    mod = jax.nn.silu(cvec) @ P['w_mod'][l] + P['b_mod'][l]
    sh1, sc1, gt1, sh2, sc2, gt2 = jnp.split(mod[:, None, :], 6, axis=-1)
    h = rms_norm(x, P['g_norm1'][l]) * (1.0 + sc1) + sh1
    mix, k, v, h_last = token_mixers(h, l, P, angles, ctx_k, ctx_v, h0)
    x = x + gt1 * mix
    h = rms_norm(x, P['g_norm2'][l]) * (1.0 + sc2) + sh2
    x = x + gt2 * conv_ffn(h, l, P)
    return x, k, v, h_last


def setup_inputs(seed: int = 0) -> dict:
    key = jax.random.key(seed)
    k = jax.random.split(key, 40)
    f32 = jnp.float32

    def nrm(kk, shape, scale):
        return jax.random.normal(kk, shape, f32) * scale

    def gain(kk, shape):
        return 1.0 + 0.02 * jax.random.normal(kk, shape, f32)

    u = jax.random.uniform(k[29], (DEPTH, 2, D_MIX), f32, 0.9, 0.999)
    a_base = u ** (1.0 / LRU_C)
    lru_lambda = jnp.log(a_base) - jnp.log1p(-a_base)
    return {
        'x_prompt': nrm(k[0], (BATCH, SEQ, D_MODEL), 1.0),
        'x_sample': nrm(k[1], (DEC_BATCH, DEC_SEQ, D_MODEL), 1.0),
        'cache_k': nrm(k[2], (DEC_BATCH, DEPTH, PAST_LEN, H_A, 2, DH), 1.0),
        'cache_v': nrm(k[3], (DEC_BATCH, DEPTH, PAST_LEN, H_A, V_DIM), 1.0),
        'state_lru': nrm(k[4], (DEC_BATCH, DEPTH, 2, D_MIX), 0.5),
        'c': nrm(k[5], (DEC_BATCH, D_MODEL), 1.0),
        'c_ctx': nrm(k[6], (D_MODEL,), 1.0),
        'w_mod': nrm(k[7], (DEPTH, D_MODEL, 6 * D_MODEL), 0.5 * D_MODEL ** -0.5),
        'b_mod': nrm(k[8], (DEPTH, 6 * D_MODEL), 0.02),
        'g_norm1': gain(k[9], (DEPTH, D_MODEL)),
        'g_norm2': gain(k[10], (DEPTH, D_MODEL)),
        'g_final': gain(k[11], (D_MODEL,)),
        'w_in': nrm(k[12], (DEPTH, D_MODEL, N_IN), D_MODEL ** -0.5),
        'lam_q1': nrm(k[13], (DEPTH, DH), 0.1),
        'lam_k1': nrm(k[14], (DEPTH, DH), 0.1),
        'lam_q2': nrm(k[15], (DEPTH, DH), 0.1),
        'lam_k2': nrm(k[16], (DEPTH, DH), 0.1),
        'g_subln': gain(k[17], (DEPTH, V_DIM)),
        'w_dw31': nrm(k[18], (DEPTH, CONV_B, D_MIX), CONV_B ** -0.5),
        'b_dw31': nrm(k[19], (DEPTH, D_MIX), 0.02),
        'g_ln_conv': gain(k[20], (DEPTH, D_MIX)),
        'b_ln_conv': nrm(k[21], (DEPTH, D_MIX), 0.02),
        'w_dw3': nrm(k[22], (DEPTH, CONV_C, D_MIX), CONV_C ** -0.5),
        'w_conv4': nrm(k[23], (DEPTH, CONV_R, D_MIX), CONV_R ** -0.5),
        'b_conv4': nrm(k[24], (DEPTH, D_MIX), 0.02),
        'w_rg_a': nrm(k[25], (DEPTH, 2, NB_R, BS_R, BS_R), BS_R ** -0.5),
        'b_rg_a': nrm(k[26], (DEPTH, 2, D_MIX), 0.02),
        'w_rg_x': nrm(k[27], (DEPTH, 2, NB_R, BS_R, BS_R), BS_R ** -0.5),
        'b_rg_x': nrm(k[28], (DEPTH, 2, D_MIX), 0.02),
        'lru_lambda': lru_lambda,
        'w_branch': nrm(k[30], (DEPTH, N_BRANCH, D_MIX, D_MODEL), D_MIX ** -0.5),
        'w_out': nrm(k[31], (DEPTH, D_MODEL, D_MODEL), D_MODEL ** -0.5),
        'w_ffn_up': nrm(k[32], (DEPTH, D_MODEL, 2 * D_FF), D_MODEL ** -0.5),
        'w_ffn_conv': nrm(k[33], (DEPTH, CONV_FF, D_FF), CONV_FF ** -0.5),
        'b_ffn_conv': nrm(k[34], (DEPTH, D_FF), 0.02),
        'w_ffn_down': nrm(k[35], (DEPTH, D_FF, D_MODEL), D_FF ** -0.5),
    }


def reference(x_prompt, x_sample, cache_k, cache_v, state_lru, c, c_ctx, w_mod, b_mod,
              g_norm1, g_norm2, g_final, w_in, lam_q1, lam_k1, lam_q2, lam_k2, g_subln,
              w_dw31, b_dw31, g_ln_conv, b_ln_conv, w_dw3, w_conv4, b_conv4,
              w_rg_a, b_rg_a, w_rg_x, b_rg_x, lru_lambda, w_branch, w_out,
              w_ffn_up, w_ffn_conv, b_ffn_conv, w_ffn_down):
    P = dict(w_mod=w_mod, b_mod=b_mod, g_norm1=g_norm1, g_norm2=g_norm2, w_in=w_in,
             lam_q1=lam_q1, lam_k1=lam_k1, lam_q2=lam_q2, lam_k2=lam_k2, g_subln=g_subln,
             w_dw31=w_dw31, b_dw31=b_dw31, g_ln_conv=g_ln_conv, b_ln_conv=b_ln_conv,
             w_dw3=w_dw3, w_conv4=w_conv4, b_conv4=b_conv4, w_rg_a=w_rg_a, b_rg_a=b_rg_a,
             w_rg_x=w_rg_x, b_rg_x=b_rg_x, lru_lambda=lru_lambda, w_branch=w_branch,
             w_out=w_out, w_ffn_up=w_ffn_up, w_ffn_conv=w_ffn_conv, b_ffn_conv=b_ffn_conv,
             w_ffn_down=w_ffn_down)

    xp = x_prompt
    c_ctx_b = c_ctx[None, :]
    ks, vs, ss = [], [], []
    for l in range(DEPTH):
        xp, k_l, v_l, s_l = trunk_layer(xp, c_ctx_b, l, P, None, None, None, None)
        ks.append(k_l)
        vs.append(v_l)
        ss.append(s_l)
    y_prompt = rms_norm(xp, g_final)
    new_cache_k = jnp.stack(ks, axis=1)
    new_cache_v = jnp.stack(vs, axis=1)
    new_state_lru = jnp.stack(ss, axis=1)

    angles = grid_angles(x_sample.shape[1])
    xs = x_sample
    for l in range(DEPTH):
        xs, _, _, _ = trunk_layer(xs, c, l, P, angles, cache_k[:, l], cache_v[:, l], state_lru[:, l])
    y_sample = rms_norm(xs, g_final)
    return (y_prompt, y_sample, new_cache_k, new_cache_v, new_state_lru)
```

```python
import functools
import math

import jax
import jax.numpy as jnp
import numpy as np
from jax import lax
from jax.experimental import pallas as pl
from jax.experimental.pallas import tpu as pltpu

F32 = jnp.float32
BF16 = jnp.bfloat16

EPS = 1e-6
ROPE_BASE = 10000.0
GRID_W = 64
LRU_C = 8.0
V_DIM = 128
DH = V_DIM // 2
BS_R = 64
LANES = 128
SUBLANES = 8
VMEM_LIMIT_BYTES = 56 * 1024 * 1024


def _params(*sem):
    return pltpu.CompilerParams(dimension_semantics=sem, vmem_limit_bytes=VMEM_LIMIT_BYTES)


def _block(n, pref):
    b = min(n, pref)
    while n % b:
        b //= 2
    return b


def _sigmoid(x):
    return 1.0 / (1.0 + jnp.exp(-x))


def _silu(x):
    return x * _sigmoid(x)


def _gelu_tanh(x):
    return 0.5 * x * (1.0 + jnp.tanh(math.sqrt(2.0 / math.pi) * (x + 0.044715 * (x * x * x))))


def _row_shift(win, off, rows):
    n = win.shape[0]
    sub = off % SUBLANES
    base = off - sub
    if sub:
        win = pltpu.roll(win, n - sub, 0)
    return win[base:base + rows]


def _mod_kernel(c_ref, w_ref, b_ref, o_ref):
    c = c_ref[...]
    s = _silu(c).astype(BF16)
    o_ref[...] = jnp.dot(s, w_ref[...].astype(BF16), preferred_element_type=F32) + b_ref[...]


def _modulation(cvec, w_mod, b_mod):
    depth, d, n6 = w_mod.shape
    rows = cvec.shape[0]
    bn = _block(n6, 1024)
    return pl.pallas_call(
        _mod_kernel,
        grid=(depth, n6 // bn),
        in_specs=[
            pl.BlockSpec((rows, d), lambda l, n: (0, 0)),
            pl.BlockSpec((None, d, bn), lambda l, n: (l, 0, n)),
            pl.BlockSpec((None, 1, bn), lambda l, n: (l, 0, n)),
        ],
        out_specs=pl.BlockSpec((None, rows, bn), lambda l, n: (l, 0, n)),
        out_shape=jax.ShapeDtypeStruct((depth, rows, n6), F32),
        compiler_params=_params("arbitrary", "arbitrary"),
        name="modulation",
    )(cvec, w_mod, b_mod.reshape(depth, 1, n6))


def _mod_spec(layer, which, row_fn, bn=None, col=False):
    def full(i, j):
        return (layer, row_fn(i), which, 0, 0)

    def cols(i, j):
        return (layer, row_fn(i), which, 0, j)

    if col:
        return pl.BlockSpec((None, None, None, 1, bn), cols)
    return pl.BlockSpec((None, None, None, 1, bn), full)


def _rope_mix(acc, cos, sin):
    reps = acc.shape[1] // LANES
    cos = jnp.concatenate([cos] * reps, axis=1)
    sin = jnp.concatenate([sin] * reps, axis=1)
    w = acc.shape[1]
    lane = lax.broadcasted_iota(jnp.int32, acc.shape, 1)
    first_half = (lane % 32) < 16
    up = pltpu.roll(acc, w - 16, 1)
    down = pltpu.roll(acc, 16, 1)
    return acc * cos + jnp.where(first_half, up, down) * sin


def _in_kernel(*refs, rope, kv32):
    x_ref, g_ref, sc_ref, sh_ref, w_ref = refs[:5]
    pos = 5
    if rope:
        cos_ref, sin_ref = refs[pos:pos + 2]
        pos += 2
    h_out, q_out, k_out, v_out = refs[pos:pos + 4]
    pos += 4
    if kv32:
        kv_out = refs[pos]
        pos += 1
    z_out, h_scr = refs[pos:pos + 2]

    j = pl.program_id(1)

    @pl.when(j == 0)
    def _():
        x = x_ref[...]
        y = x * lax.rsqrt(jnp.mean(x * x, axis=-1, keepdims=True) + EPS) * g_ref[...]
        h = (y * (1.0 + sc_ref[...]) + sh_ref[...]).astype(BF16)
        h_scr[...] = h
        h_out[...] = h

    acc = jnp.dot(h_scr[...], w_ref[...], preferred_element_type=F32)

    @pl.when(j == 0)
    def _():
        q = _rope_mix(acc, cos_ref[...], sin_ref[...]) if rope else acc
        q_out[...] = (q * (DH ** -0.5)).astype(BF16)

    @pl.when(j == 1)
    def _():
        k = _rope_mix(acc, cos_ref[...], sin_ref[...]) if rope else acc
        k_out[...] = k.astype(BF16)
        if kv32:
            kv_out[...] = k

    @pl.when(j == 2)
    def _():
        v_out[...] = acc.astype(BF16)
        if kv32:
            kv_out[...] = acc

    @pl.when(j >= 3)
    def _():
        z_out[...] = acc


def _in_projection(x, mod, layer, row_of, g_norm, w_in, tables, seq, kv32):
    t, d = x.shape
    dm = d // 4
    ncol = w_in.shape[1]
    nz = ncol // dm - 3
    bm = _block(seq, 512)
    row_fn = row_of(bm)
    nblk_seq = seq // bm
    rope = tables is not None
    in_specs = [
        pl.BlockSpec((bm, d), lambda i, j: (i, 0)),
        pl.BlockSpec((None, 1, d), lambda i, j: (layer, 0, 0)),
        _mod_spec(layer, 1, row_fn, d),
        _mod_spec(layer, 0, row_fn, d),
        pl.BlockSpec((d, dm), lambda i, j: (0, j)),
    ]
    args = [x, g_norm, mod, mod, w_in]
    if rope:
        in_specs += [pl.BlockSpec((bm, LANES), lambda i, j: (i % nblk_seq, 0))] * 2
        args += list(tables)
    out_specs = [
        pl.BlockSpec((bm, d), lambda i, j: (i, 0)),
        pl.BlockSpec((bm, dm), lambda i, j: (i, 0)),
        pl.BlockSpec((bm, dm), lambda i, j: (i, 0)),
        pl.BlockSpec((bm, dm), lambda i, j: (i, 0)),
    ]
    out_shape = [
        jax.ShapeDtypeStruct((t, d), BF16),
        jax.ShapeDtypeStruct((t, dm), BF16),
        jax.ShapeDtypeStruct((t, dm), BF16),
        jax.ShapeDtypeStruct((t, dm), BF16),
    ]
    if kv32:
        out_specs.append(pl.BlockSpec((bm, dm), lambda i, j: (i, jnp.clip(j - 1, 0, 1))))
        out_shape.append(jax.ShapeDtypeStruct((t, 2 * dm), F32))
    out_specs.append(pl.BlockSpec((bm, dm), lambda i, j: (i, jnp.clip(j - 3, 0, nz - 1))))
    out_shape.append(jax.ShapeDtypeStruct((t, nz * dm), F32))
    return pl.pallas_call(
        functools.partial(_in_kernel, rope=rope, kv32=kv32),
        grid=(t // bm, ncol // dm),
        in_specs=in_specs,
        out_specs=out_specs,
        out_shape=out_shape,
        scratch_shapes=[pltpu.VMEM((bm, d), BF16)],
        compiler_params=_params("arbitrary", "arbitrary"),
        name="in_projection",
    )(*args)


def _attn_kernel(*refs, has_ctx, lam_init):
    q_ref, k_ref, v_ref = refs[:3]
    pos = 3
    if has_ctx:
        ck_ref, cv_ref = refs[pos:pos + 2]
        pos += 2
    lq1, lk1, lq2, lk2, g_ref, o_ref = refs[pos:pos + 6]

    lam = (jnp.exp(jnp.sum(lq1[...] * lk1[...], axis=-1, keepdims=True))
           - jnp.exp(jnp.sum(lq2[...] * lk2[...], axis=-1, keepdims=True)) + lam_init)

    q = q_ref[...]
    k = k_ref[...]
    lane = lax.broadcasted_iota(jnp.int32, q.shape, 1)
    nt = (((1,), (1,)), ((), ()))
    if has_ctx:
        ck = ck_ref[...].astype(BF16)
        cv = cv_ref[...].astype(BF16)

    es, ls = [], []
    for m in range(2):
        in_map = (lane >= m * DH) & (lane < (m + 1) * DH)
        qm = jnp.where(in_map, q, jnp.zeros_like(q))
        s = [lax.dot_general(qm, k, nt, preferred_element_type=F32)]
        if has_ctx:
            s.append(lax.dot_general(qm, ck, nt, preferred_element_type=F32))
        mx = functools.reduce(jnp.maximum, [jnp.max(si, axis=-1, keepdims=True) for si in s])
        e = [jnp.exp(si - mx) for si in s]
        ls.append(functools.reduce(lambda a, b: a + b, [jnp.sum(ei, axis=-1, keepdims=True) for ei in e]))
        es.append(e)

    c0 = 1.0 / ls[0]
    c1 = lam / ls[1]
    o = jnp.dot((es[0][0] * c0 - es[1][0] * c1).astype(BF16), v_ref[...], preferred_element_type=F32)
    if has_ctx:
        o = o + jnp.dot((es[0][1] * c0 - es[1][1] * c1).astype(BF16), cv, preferred_element_type=F32)

    y = o * lax.rsqrt(jnp.mean(o * o, axis=-1, keepdims=True) + EPS) * g_ref[...]
    o_ref[...] = (y * (1.0 - lam_init)).astype(o_ref.dtype)


def _attention(q, k, v, ctx_k, ctx_v, layer, lam_p, g_subln):
    b, n, dm = q.shape
    heads = dm // V_DIM
    bq = _block(n, 256)
    has_ctx = ctx_k is not None
    lam_init = 0.8 - 0.6 * math.exp(-0.3 * layer)
    in_specs = [
        pl.BlockSpec((None, bq, V_DIM), lambda bi, h, i: (bi, i, h)),
        pl.BlockSpec((None, n, V_DIM), lambda bi, h, i: (bi, 0, h)),
        pl.BlockSpec((None, n, V_DIM), lambda bi, h, i: (bi, 0, h)),
    ]
    args = [q, k, v]
    if has_ctx:
        past = ctx_k.shape[2]
        in_specs += [pl.BlockSpec((None, None, past, V_DIM), lambda bi, h, i: (bi, layer, 0, h))] * 2
        args += [ctx_k, ctx_v]
    in_specs += [pl.BlockSpec((None, 1, DH), lambda bi, h, i: (layer, 0, 0))] * 4
    in_specs += [pl.BlockSpec((None, 1, V_DIM), lambda bi, h, i: (layer, 0, 0))]
    args += list(lam_p) + [g_subln]
    return pl.pallas_call(
        functools.partial(_attn_kernel, has_ctx=has_ctx, lam_init=lam_init),
        grid=(b, heads, n // bq),
        in_specs=in_specs,
        out_specs=pl.BlockSpec((None, bq, V_DIM), lambda bi, h, i: (bi, i, h)),
        out_shape=jax.ShapeDtypeStruct((b, n, dm), BF16),
        compiler_params=_params("arbitrary", "arbitrary", "arbitrary"),
        name="diff_attention",
    )(*args)


def _halo_specs(bn, halo, width, col, nseq_halo):
    per = bn // halo
    return [
        pl.BlockSpec((None, halo, width), lambda b, i: (b, jnp.maximum(i * per - 1, 0), col)),
        pl.BlockSpec((None, bn, width), lambda b, i: (b, i, col)),
        pl.BlockSpec((None, halo, width), lambda b, i: (b, jnp.minimum((i + 1) * per, nseq_halo - 1), col)),
    ]


def _edge_masks():
    i = pl.program_id(1)
    return i > 0, i < pl.num_programs(1) - 1


def _convmod_kernel(p_ref, m_ref, n_ref, w_ref, b_ref, g_ref, bl_ref, o_ref, *, halo):
    dm = o_ref.shape[-1]
    bn = o_ref.shape[0]
    taps = w_ref.shape[0]
    has_prev, has_next = _edge_masks()

    def glu(ref):
        z = ref[...]
        return z[:, :dm] * _sigmoid(z[:, dm:])

    prev = jnp.where(has_prev, glu(p_ref), 0.0)
    nxt = jnp.where(has_next, glu(n_ref), 0.0)
    win = jnp.concatenate([prev, glu(m_ref), nxt], axis=0)
    w = w_ref[...]
    left = (taps - 1) // 2
    acc = jnp.zeros((bn, dm), F32) + b_ref[...]
    shifted = {}
    for t in range(taps):
        off = halo + t - left
        sub = off % SUBLANES
        if sub not in shifted:
            shifted[sub] = pltpu.roll(win, win.shape[0] - sub, 0) if sub else win
        base = off - sub
        acc = acc + shifted[sub][base:base + bn] * w[t:t + 1]
    mu = jnp.mean(acc, axis=-1, keepdims=True)
    xc = acc - mu
    y = xc * lax.rsqrt(jnp.mean(xc * xc, axis=-1, keepdims=True) + EPS) * g_ref[...] + bl_ref[...]
    o_ref[...] = _silu(y).astype(o_ref.dtype)


def _conv_module(z, layer, w_dw, b_dw, g_ln, b_ln):
    b, n, _ = z.shape
    taps, dm = w_dw.shape[1:]
    halo = 16
    bn = _block(n, 256)
    vec = pl.BlockSpec((None, 1, dm), lambda bi, i: (layer, 0, 0))
    return pl.pallas_call(
        functools.partial(_convmod_kernel, halo=halo),
        grid=(b, n // bn),
        in_specs=_halo_specs(bn, halo, 2 * dm, 0, n // halo)
        + [pl.BlockSpec((None, taps, dm), lambda bi, i: (layer, 0, 0)), vec, vec, vec],
        out_specs=pl.BlockSpec((None, bn, dm), lambda bi, i: (bi, i, 0)),
        out_shape=jax.ShapeDtypeStruct((b, n, dm), BF16),
        compiler_params=_params("arbitrary", "arbitrary"),
        name="conformer_conv",
    )(z, z, z, w_dw, b_dw, g_ln, b_ln)


def _shortconv_kernel(p_ref, m_ref, n_ref, gb_ref, w_ref, o_ref, *, halo):
    dm = o_ref.shape[-1]
    bn = o_ref.shape[0]
    taps = w_ref.shape[0]
    has_prev, has_next = _edge_masks()

    def prod(ref):
        z = ref[...]
        return z[:, :dm] * z[:, dm:]

    prev = jnp.where(has_prev, prod(p_ref), 0.0)
    nxt = jnp.where(has_next, prod(n_ref), 0.0)
    win = jnp.concatenate([prev, prod(m_ref), nxt], axis=0)
    w = w_ref[...]
    left = (taps - 1) // 2
    acc = jnp.zeros((bn, dm), F32)
    for t in range(taps):
        acc = acc + _row_shift(win, halo + t - left, bn) * w[t:t + 1]
    o_ref[...] = (gb_ref[...] * acc).astype(o_ref.dtype)


def _short_conv(z, layer, w_dw):
    b, n, _ = z.shape
    taps, dm = w_dw.shape[1:]
    halo = SUBLANES
    bn = _block(n, 512)
    return pl.pallas_call(
        functools.partial(_shortconv_kernel, halo=halo),
        grid=(b, n // bn),
        in_specs=_halo_specs(bn, halo, 2 * dm, 1, n // halo)
        + [pl.BlockSpec((None, bn, dm), lambda bi, i: (bi, i, 4)),
           pl.BlockSpec((None, taps, dm), lambda bi, i: (layer, 0, 0))],
        out_specs=pl.BlockSpec((None, bn, dm), lambda bi, i: (bi, i, 0)),
        out_shape=jax.ShapeDtypeStruct((b, n, dm), BF16),
        compiler_params=_params("arbitrary", "arbitrary"),
        name="short_conv",
    )(z, z, z, z, w_dw)


def _scan_block(a, u, reverse):
    rows = a.shape[0]
    row = lax.broadcasted_iota(jnp.int32, a.shape, 0)
    s = 1
    while s < rows:
        if reverse:
            a_sh = pltpu.roll(a, rows - s, 0)
            u_sh = pltpu.roll(u, rows - s, 0)
            ok = row < rows - s
        else:
            a_sh = pltpu.roll(a, s, 0)
            u_sh = pltpu.roll(u, s, 0)
            ok = row >= s
        u = jnp.where(ok, a * u_sh + u, u)
        a = jnp.where(ok, a * a_sh, a)
        s *= 2
    return a, u


def _rglru_kernel(x_ref, y_ref, h0_ref, wc_ref, bc_ref, wg_ref, bg_ref, lam_ref,
                  o_ref, hl_ref, xpad, af, uf, ab, ub, *, rows):
    n, cw = x_ref.shape
    nchunk = n // rows
    pad = SUBLANES
    taps = wc_ref.shape[0]
    left = (taps - 1) // 2

    xpad[0:pad, :] = jnp.zeros((pad, cw), F32)
    xpad[pad:pad + n, :] = x_ref[...]
    xpad[pad + n:pad + n + pad, :] = jnp.zeros((pad, cw), F32)

    lam = lam_ref[...]
    softplus = jnp.maximum(-lam, 0.0) + jnp.log1p(jnp.exp(-jnp.abs(lam)))
    wc = wc_ref[...]
    bc = bc_ref[...]
    wg = wg_ref[...]
    bg = bg_ref[...]

    def gates(ci, carry):
        r0 = pl.multiple_of(ci * rows, rows)
        win = xpad[pl.ds(r0, rows + 2 * pad), :]
        xr = jnp.zeros((rows, cw), F32) + bc
        for t in range(taps):
            xr = xr + _row_shift(win, pad + t - left, rows) * wc[t:t + 1]
        g = jnp.dot(xr.astype(BF16), wg, preferred_element_type=F32) + bg
        for d, (a_ref, u_ref) in enumerate(((af, uf), (ab, ub))):
            r = _sigmoid(g[:, (2 * d) * cw:(2 * d + 1) * cw])
            i = _sigmoid(g[:, (2 * d + 1) * cw:(2 * d + 2) * cw])
            a = jnp.exp(-LRU_C * r * softplus[:, d * cw:(d + 1) * cw])
            u = jnp.sqrt(jnp.maximum(1.0 - a * a, 0.0)) * (i * xr)
            a_ref[pl.ds(r0, rows), :] = a
            u_ref[pl.ds(r0, rows), :] = u
        return carry

    lax.fori_loop(0, nchunk, gates, 0)

    def forward(ci, h):
        r0 = pl.multiple_of(ci * rows, rows)
        p, hz = _scan_block(af[pl.ds(r0, rows), :], uf[pl.ds(r0, rows), :], False)
        hs = hz + p * h
        uf[pl.ds(r0, rows), :] = hs
        return hs[rows - 1:rows]

    h_f = lax.fori_loop(0, nchunk, forward, h0_ref[0:1, :])

    def backward(cj, h):
        r0 = pl.multiple_of((nchunk - 1 - cj) * rows, rows)
        p, hz = _scan_block(ab[pl.ds(r0, rows), :], ub[pl.ds(r0, rows), :], True)
        hs = hz + p * h
        yr = y_ref[pl.ds(r0, rows), :]
        o_ref[pl.ds(r0, rows), :] = ((uf[pl.ds(r0, rows), :] + hs) * _gelu_tanh(yr)).astype(o_ref.dtype)
        return hs[0:1]

    h_b = lax.fori_loop(0, nchunk, backward, h0_ref[1:2, :])
    hl_ref[0:1, :] = h_f
    hl_ref[1:2, :] = h_b


def _rglru(z, h0, layer, w_conv4, b_conv4, wg_bd, bg_bd, lam_bd):
    b, n, zc = z.shape
    dm = w_conv4.shape[-1]
    cw = LANES
    nck = dm // cw
    taps = w_conv4.shape[1]
    rows = _block(n, 256)
    xcol = 5 * nck
    ycol = 6 * nck
    seq = pltpu.VMEM((n, cw), F32)
    return pl.pallas_call(
        functools.partial(_rglru_kernel, rows=rows),
        grid=(b, nck),
        in_specs=[
            pl.BlockSpec((None, n, cw), lambda bi, c: (bi, 0, xcol + c)),
            pl.BlockSpec((None, n, cw), lambda bi, c: (bi, 0, ycol + c)),
            pl.BlockSpec((None, 2, cw), lambda bi, c: (bi, 0, c)),
            pl.BlockSpec((None, taps, cw), lambda bi, c: (layer, 0, c)),
            pl.BlockSpec((None, 1, cw), lambda bi, c: (layer, 0, c)),
            pl.BlockSpec((None, None, cw, 4 * cw), lambda bi, c: (layer, c, 0, 0)),
            pl.BlockSpec((None, None, 1, 4 * cw), lambda bi, c: (layer, c, 0, 0)),
            pl.BlockSpec((None, None, 1, 2 * cw), lambda bi, c: (layer, c, 0, 0)),
        ],
        out_specs=[
            pl.BlockSpec((None, n, cw), lambda bi, c: (bi, 0, c)),
            pl.BlockSpec((None, 2, cw), lambda bi, c: (bi, 0, c)),
        ],
        out_shape=[
            jax.ShapeDtypeStruct((b, n, dm), BF16),
            jax.ShapeDtypeStruct((b, 2, dm), F32),
        ],
        scratch_shapes=[pltpu.VMEM((n + 2 * SUBLANES, cw), F32), seq, seq, seq, seq],
        compiler_params=_params("arbitrary", "arbitrary"),
        name="rglru",
    )(z, z, h0, w_conv4, b_conv4, wg_bd, bg_bd, lam_bd)


def _merge_kernel(h_ref, ya, yb, yc, yd, g0, g1, g2, g3, b0, b1, b2, b3, o_ref):
    h = h_ref[...]
    acc = None
    for y_ref, wg_ref, wb_ref in ((ya, g0, b0), (yb, g1, b1), (yc, g2, b2), (yd, g3, b3)):
        gate = _sigmoid(jnp.dot(h, wg_ref[...], preferred_element_type=F32))
        term = gate * jnp.dot(y_ref[...], wb_ref[...], preferred_element_type=F32)
        acc = term if acc is None else acc + term
    o_ref[...] = acc.astype(o_ref.dtype)


def _merge(h, ys, w_gate, w_branch, layer, seq):
    t, d = h.shape
    dm = d // 4
    bm = _block(seq, 1024)
    bn = _block(d, 256)
    ncb = d // bn
    in_specs = [pl.BlockSpec((bm, d), lambda i, j: (i, 0))]
    in_specs += [pl.BlockSpec((bm, dm), lambda i, j: (i, 0))] * 4
    in_specs += [pl.BlockSpec((d, bn), functools.partial(lambda i, j, jb: (0, jb * ncb + j), jb=jb))
                 for jb in range(4)]
    in_specs += [pl.BlockSpec((None, None, dm, bn), functools.partial(lambda i, j, jb: (layer, jb, 0, j), jb=jb))
                 for jb in range(4)]
    return pl.pallas_call(
        _merge_kernel,
        grid=(t // bm, ncb),
        in_specs=in_specs,
        out_specs=pl.BlockSpec((bm, bn), lambda i, j: (i, j)),
        out_shape=jax.ShapeDtypeStruct((t, d), BF16),
        compiler_params=_params("arbitrary", "arbitrary"),
        name="gated_merge",
    )(h, *ys, *([w_gate] * 4), *([w_branch] * 4))


def _resid_kernel(a_ref, w_ref, x_ref, gt_ref, o_ref):
    o_ref[...] = x_ref[...] + gt_ref[...] * jnp.dot(a_ref[...], w_ref[...], preferred_element_type=F32)


def _residual_proj(a, w, x, mod, layer, which, row_of, seq, name):
    t, kdim = a.shape
    d = x.shape[1]
    bm = _block(seq, 512)
    bn = _block(d, 512)
    row_fn = row_of(bm)
    return pl.pallas_call(
        _resid_kernel,
        grid=(t // bm, d // bn),
        in_specs=[
            pl.BlockSpec((bm, kdim), lambda i, j: (i, 0)),
            pl.BlockSpec((kdim, bn), lambda i, j: (0, j)),
            pl.BlockSpec((bm, bn), lambda i, j: (i, j)),
            _mod_spec(layer, which, row_fn, bn, col=True),
        ],
        out_specs=pl.BlockSpec((bm, bn), lambda i, j: (i, j)),
        out_shape=jax.ShapeDtypeStruct((t, d), F32),
        compiler_params=_params("arbitrary", "arbitrary"),
        name=name,
    )(a, w, x, mod)


def _ffn_up_kernel(p_ref, m_ref, n_ref, g_ref, sc_ref, sh_ref, wa_ref, wu_ref, wc_ref, bc_ref,
                   o_ref, h_scr, *, nblk_seq):
    bm = m_ref.shape[0]
    halo = p_ref.shape[0]
    i = pl.program_id(0)
    j = pl.program_id(1)

    @pl.when(j == 0)
    def _():
        def norm(ref):
            x = ref[...]
            y = x * lax.rsqrt(jnp.mean(x * x, axis=-1, keepdims=True) + EPS) * g_ref[...]
            return (y * (1.0 + sc_ref[...]) + sh_ref[...]).astype(BF16)

        h_scr[0:halo, :] = norm(p_ref)
        h_scr[halo:halo + bm, :] = norm(m_ref)
        h_scr[halo + bm:halo + bm + halo, :] = norm(n_ref)

    h = h_scr[...]
    a = jnp.dot(h, wa_ref[...], preferred_element_type=F32)
    u = jnp.dot(h[halo:halo + bm], wu_ref[...], preferred_element_type=F32)
    pos = i % nblk_seq
    lo = jnp.where(pos > 0, 0, halo)
    hi = jnp.where(pos < nblk_seq - 1, bm + 2 * halo, bm + halo)
    row = lax.broadcasted_iota(jnp.int32, a.shape, 0)
    a = jnp.where((row >= lo) & (row < hi), a, 0.0)
    w = wc_ref[...]
    taps = w.shape[0]
    left = (taps - 1) // 2
    acc = jnp.zeros(u.shape, F32) + bc_ref[...]
    for t in range(taps):
        acc = acc + _row_shift(a, halo + t - left, bm) * w[t:t + 1]
    o_ref[...] = (_silu(acc) * u).astype(o_ref.dtype)


def _ffn_up(x, mod, layer, row_of, g_norm, wa, wu, w_conv, b_conv, seq):
    t, d = x.shape
    ffp = wa.shape[-1]
    taps = w_conv.shape[1]
    halo = 16
    bm = _block(seq, 512)
    bn = _block(ffp, 512)
    row_fn = row_of(bm)
    nblk_seq = seq // bm
    per = bm // halo
    nhalo = t // halo
    return pl.pallas_call(
        functools.partial(_ffn_up_kernel, nblk_seq=nblk_seq),
        grid=(t // bm, ffp // bn),
        in_specs=[
            pl.BlockSpec((halo, d), lambda i, j: (jnp.maximum(i * per - 1, 0), 0)),
            pl.BlockSpec((bm, d), lambda i, j: (i, 0)),
            pl.BlockSpec((halo, d), lambda i, j: (jnp.minimum((i + 1) * per, nhalo - 1), 0)),
            pl.BlockSpec((None, 1, d), lambda i, j: (layer, 0, 0)),
            _mod_spec(layer, 4, row_fn, d),
            _mod_spec(layer, 3, row_fn, d),
            pl.BlockSpec((None, d, bn), lambda i, j: (layer, 0, j)),
            pl.BlockSpec((None, d, bn), lambda i, j: (layer, 0, j)),
            pl.BlockSpec((None, taps, bn), lambda i, j: (layer, 0, j)),
            pl.BlockSpec((None, 1, bn), lambda i, j: (layer, 0, j)),
        ],
        out_specs=pl.BlockSpec((bm, bn), lambda i, j: (i, j)),
        out_shape=jax.ShapeDtypeStruct((t, ffp), BF16),
        scratch_shapes=[pltpu.VMEM((bm + 2 * halo, d), BF16)],
        compiler_params=_params("arbitrary", "arbitrary"),
        name="ffn_up",
    )(x, x, x, g_norm, mod, mod, wa, wu, w_conv, b_conv)


def _final_norm_kernel(x_ref, g_ref, o_ref):
    x = x_ref[...]
    o_ref[...] = x * lax.rsqrt(jnp.mean(x * x, axis=-1, keepdims=True) + EPS) * g_ref[...]


def _final_norm(x, g):
    t, d = x.shape
    bm = _block(t, 512)
    return pl.pallas_call(
        _final_norm_kernel,
        grid=(t // bm,),
        in_specs=[pl.BlockSpec((bm, d), lambda i: (i, 0)), pl.BlockSpec((1, d), lambda i: (0, 0))],
        out_specs=pl.BlockSpec((bm, d), lambda i: (i, 0)),
        out_shape=jax.ShapeDtypeStruct((t, d), F32),
        compiler_params=_params("arbitrary"),
        name="final_norm",
    )(x, g.reshape(1, d))


def _rope_tables(n):
    pairs = DH // 4
    rows = n // GRID_W
    t_row = jnp.repeat(jnp.arange(rows), GRID_W).astype(F32)
    t_col = jnp.tile(jnp.arange(GRID_W), rows).astype(F32)
    inv = jnp.power(ROPE_BASE, -jnp.arange(pairs, dtype=F32) / pairs)
    ang_row = t_row[:, None] * inv
    ang_col = t_col[:, None] * inv
    ang = jnp.concatenate([ang_row, ang_row, ang_col, ang_col], axis=1)
    sign = np.tile(np.repeat(np.array([-1.0, 1.0], np.float32), pairs), 2)
    cos = jnp.cos(ang)
    sin = jnp.sin(ang) * sign
    reps = LANES // DH
    return jnp.tile(cos, (1, reps)), jnp.tile(sin, (1, reps))


def _block_diag_gates(w_rg_a, b_rg_a, w_rg_x, b_rg_x, lru_lambda, dm):
    depth = w_rg_a.shape[0]
    cw = LANES
    nck = dm // cw
    per = cw // BS_R

    def dense(w):
        w = w.reshape(depth, 2, nck, per, BS_R, BS_R)
        eye = jnp.eye(per, dtype=w.dtype)
        full = jnp.einsum('ldcpij,pq->ldcpiqj', w, eye)
        return full.reshape(depth, 2, nck, cw, cw)

    wa, wx = dense(w_rg_a), dense(w_rg_x)
    wg = jnp.concatenate([wa[:, 0], wx[:, 0], wa[:, 1], wx[:, 1]], axis=-1).astype(BF16)

    def vec(v):
        return v.reshape(depth, 2, nck, 1, cw)

    ba, bx = vec(b_rg_a), vec(b_rg_x)
    bg = jnp.concatenate([ba[:, 0], bx[:, 0], ba[:, 1], bx[:, 1]], axis=-1)
    lam = vec(lru_lambda)
    lam = jnp.concatenate([lam[:, 0], lam[:, 1]], axis=-1)
    return wg, bg, lam


def kernel(x_prompt, x_sample, cache_k, cache_v, state_lru, c, c_ctx, w_mod, b_mod, g_norm1, g_norm2, g_final, w_in, lam_q1, lam_k1, lam_q2, lam_k2, g_subln, w_dw31, b_dw31, g_ln_conv, b_ln_conv, w_dw3, w_conv4, b_conv4, w_rg_a, b_rg_a, w_rg_x, b_rg_x, lru_lambda, w_branch, w_out, w_ffn_up, w_ffn_conv, b_ffn_conv, w_ffn_down):
    depth, d, _ = w_in.shape
    dm = d // 4
    bc, sc, _ = x_prompt.shape
    bd, sd, _ = x_sample.shape
    past = cache_k.shape[2]
    d_ff = w_ffn_conv.shape[-1]
    ffp = -(-d_ff // 512) * 512

    rows = -(-(bd + 1) // SUBLANES) * SUBLANES
    cvec = jnp.zeros((rows, d), F32).at[:bd].set(c).at[bd].set(c_ctx)
    mod = _modulation(cvec, w_mod, b_mod).reshape(depth, rows, 6, 1, d)

    qkvz = jnp.concatenate([w_in[:, :, :5 * dm], w_in[:, :, 6 * dm:8 * dm], w_in[:, :, 5 * dm:6 * dm],
                            w_in[:, :, 8 * dm:10 * dm]], axis=-1).astype(BF16)
    w_gate = w_in[:, :, 10 * dm:].astype(BF16)
    w_branch_b = w_branch.astype(BF16)
    w_out_b = w_out.astype(BF16)
    padc = [(0, 0), (0, 0), (0, ffp - d_ff)]
    w_up_a = jnp.pad(w_ffn_up[:, :, :d_ff], padc).astype(BF16)
    w_up_u = jnp.pad(w_ffn_up[:, :, d_ff:], padc).astype(BF16)
    w_fc = jnp.pad(w_ffn_conv, padc)
    b_fc = jnp.pad(b_ffn_conv, [(0, 0), (0, ffp - d_ff)]).reshape(depth, 1, ffp)
    w_down = jnp.pad(w_ffn_down, [(0, 0), (0, ffp - d_ff), (0, 0)]).astype(BF16)
    wg_bd, bg_bd, lam_bd = _block_diag_gates(w_rg_a, b_rg_a, w_rg_x, b_rg_x, lru_lambda, dm)

    def v3(p):
        return p.reshape(depth, 1, p.shape[-1])

    g1, g2 = v3(g_norm1), v3(g_norm2)
    lam_p = [v3(lam_q1), v3(lam_k1), v3(lam_q2), v3(lam_k2)]
    g_sub = v3(g_subln)
    b31, gln, bln, b4 = v3(b_dw31), v3(g_ln_conv), v3(b_ln_conv), v3(b_conv4)

    def layer_pass(x, layer, b, n, row_of, tables, ctx_k, ctx_v, h0, want_kv):
        outs = _in_projection(x, mod, layer, row_of, g1, qkvz[layer], tables, n, want_kv)
        if want_kv:
            h, q, k, v, kv32, z = outs
        else:
            h, q, k, v, z = outs
            kv32 = None
        q, k, v, z = (a.reshape(b, n, -1) for a in (q, k, v, z))
        y_a = _attention(q, k, v, ctx_k, ctx_v, layer, lam_p, g_sub)
        y_b = _conv_module(z, layer, w_dw31, b31, gln, bln)
        y_c = _short_conv(z, layer, w_dw3)
        y_d, h_last = _rglru(z, h0, layer, w_conv4, b4, wg_bd, bg_bd, lam_bd)
        ys = [y.reshape(b * n, dm) for y in (y_a, y_b, y_c, y_d)]
        merged = _merge(h, ys, w_gate[layer], w_branch_b, layer, n)
        x1 = _residual_proj(merged, w_out_b[layer], x, mod, layer, 2, row_of, n, "out_projection")
        act = _ffn_up(x1, mod, layer, row_of, g2, w_up_a, w_up_u, w_fc, b_fc, n)
        x2 = _residual_proj(act, w_down[layer], x1, mod, layer, 5, row_of, n, "ffn_down")
        return x2, kv32, h_last

    xp = x_prompt.reshape(bc * sc, d)
    zero_state = jnp.zeros((bc, 2, dm), F32)
    ks, vs, ss = [], [], []
    for layer in range(depth):
        xp, kv32, h_last = layer_pass(xp, layer, bc, sc, lambda bm: (lambda i: bd), None, None, None,
                                      zero_state, True)
        kv32 = kv32.reshape(bc, sc, 2 * dm)
        ks.append(kv32[:, :, :dm])
        vs.append(kv32[:, :, dm:])
        ss.append(h_last)
    heads = dm // V_DIM
    y_prompt = _final_norm(xp, g_final).reshape(bc, sc, d)
    new_cache_k = jnp.stack(ks, axis=1).reshape(bc, depth, sc, heads, 2, DH)
    new_cache_v = jnp.stack(vs, axis=1).reshape(bc, depth, sc, heads, V_DIM)
    new_state = jnp.stack(ss, axis=1)

    tables = _rope_tables(sd)
    ck = cache_k.reshape(bd, depth, past, dm)
    cv = cache_v.reshape(bd, depth, past, dm)
    xs = x_sample.reshape(bd * sd, d)
    for layer in range(depth):
        xs, _, _ = layer_pass(xs, layer, bd, sd, lambda bm: (lambda i: (i * bm) // sd), tables, ck, cv,
                              state_lru[:, layer], False)
    y_sample = _final_norm(xs, g_final).reshape(bd, sd, d)
    return (y_prompt, y_sample, new_cache_k, new_cache_v, new_state)
```

```python
import functools
import math

import jax
import jax.numpy as jnp
import numpy as np
from jax import lax
from jax.experimental import pallas as pl
from jax.experimental.pallas import tpu as pltpu

F32 = jnp.float32
BF16 = jnp.bfloat16

EPS = 1e-6
ROPE_BASE = 10000.0
GRID_W = 64
LRU_C = 8.0
V_DIM = 128
DH = V_DIM // 2
BS_R = 64
LANES = 128
SUBLANES = 8
VMEM_LIMIT_BYTES = 56 * 1024 * 1024
ROW_BLOCK = 1024
KEY_CHUNK = 512


def _params(*sem):
    return pltpu.CompilerParams(dimension_semantics=sem, vmem_limit_bytes=VMEM_LIMIT_BYTES)


def _block(n, pref):
    b = min(n, pref)
    while n % b:
        b //= 2
    return b


def _row_blocks(rows_per_mod, base_row, pref=ROW_BLOCK):
    bm = _block(rows_per_mod, pref)
    return bm, (lambda i: base_row + (i * bm) // rows_per_mod)


def _sigmoid(x):
    return 1.0 / (1.0 + jnp.exp(-x))


def _silu(x):
    return x * _sigmoid(x)


def _gelu_tanh(x):
    return 0.5 * x * (1.0 + jnp.tanh(math.sqrt(2.0 / math.pi) * (x + 0.044715 * (x * x * x))))


def _adaln(x, g, sc, sh):
    y = x * lax.rsqrt(jnp.mean(x * x, axis=-1, keepdims=True) + EPS) * g
    return (y * (1.0 + sc) + sh).astype(BF16)


def _row_shift(win, off, rows):
    n = win.shape[0]
    sub = off % SUBLANES
    base = off - sub
    if sub:
        win = pltpu.roll(win, n - sub, 0)
    return win[base:base + rows]


def _mod_kernel(c_ref, w_ref, b_ref, o_ref):
    c = c_ref[...]
    s = _silu(c).astype(BF16)
    o_ref[...] = jnp.dot(s, w_ref[...].astype(BF16), preferred_element_type=F32) + b_ref[...]


def _modulation(cvec, w_mod, b_mod):
    depth, d, n6 = w_mod.shape
    rows = cvec.shape[0]
    bn = _block(n6, 1024)
    return pl.pallas_call(
        _mod_kernel,
        grid=(depth, n6 // bn),
        in_specs=[
            pl.BlockSpec((rows, d), lambda l, n: (0, 0)),
            pl.BlockSpec((None, d, bn), lambda l, n: (l, 0, n)),
            pl.BlockSpec((None, 1, bn), lambda l, n: (l, 0, n)),
        ],
        out_specs=pl.BlockSpec((None, rows, bn), lambda l, n: (l, 0, n)),
        out_shape=jax.ShapeDtypeStruct((depth, rows, n6), F32),
        compiler_params=_params("arbitrary", "arbitrary"),
        name="modulation",
    )(cvec, w_mod, b_mod.reshape(depth, 1, n6))


def _mod_spec(layer, which, row_fn, bn, col=False):
    def full(i, j):
        return (layer, row_fn(i), which, 0, 0)

    def cols(i, j):
        return (layer, row_fn(i), which, 0, j)

    return pl.BlockSpec((None, None, None, 1, bn), cols if col else full)


def _rope_store(out_ref, acc, cos, sin, scale):
    rows = acc.shape[0]
    lane = lax.broadcasted_iota(jnp.int32, (rows, LANES), 1)
    first_half = (lane % 32) < 16
    for c in range(acc.shape[1] // LANES):
        a = acc[:, c * LANES:(c + 1) * LANES]
        up = pltpu.roll(a, LANES - 16, 1)
        down = pltpu.roll(a, 16, 1)
        r = a * cos + jnp.where(first_half, up, down) * sin
        if scale != 1.0:
            r = r * scale
        out_ref[:, c * LANES:(c + 1) * LANES] = r.astype(out_ref.dtype)


def _in_kernel(*refs, rope, kv32):
    x_ref, g_ref, sc_ref, sh_ref, w_ref = refs[:5]
    pos = 5
    if rope:
        cos_ref, sin_ref = refs[pos:pos + 2]
        pos += 2
    q_out, k_out, v_out = refs[pos:pos + 3]
    pos += 3
    if kv32:
        kv_out = refs[pos]
        pos += 1
    z_out, h_scr = refs[pos:pos + 2]

    j = pl.program_id(1)
    q_scale = DH ** -0.5

    @pl.when(j == 0)
    def _():
        h_scr[...] = _adaln(x_ref[...], g_ref[...], sc_ref[...], sh_ref[...])

    acc = jnp.dot(h_scr[...], w_ref[...], preferred_element_type=F32)

    @pl.when(j == 0)
    def _():
        if rope:
            _rope_store(q_out, acc, cos_ref[...], sin_ref[...], q_scale)
        else:
            q_out[...] = (acc * q_scale).astype(BF16)

    @pl.when(j == 1)
    def _():
        if rope:
            _rope_store(k_out, acc, cos_ref[...], sin_ref[...], 1.0)
        else:
            k_out[...] = acc.astype(BF16)
        if kv32:
            kv_out[...] = acc

    @pl.when(j == 2)
    def _():
        v_out[...] = acc.astype(BF16)
        if kv32:
            kv_out[...] = acc

    @pl.when(j >= 3)
    def _():
        z_out[...] = acc


def _in_projection(x, mod, layer, rows_per_mod, base_row, g_norm, w_in, tables, seq, kv32):
    t, d = x.shape
    dm = d // 4
    ncol = w_in.shape[1]
    nz = ncol // dm - 3
    bm, row_fn = _row_blocks(rows_per_mod, base_row)
    rope = tables is not None
    in_specs = [
        pl.BlockSpec((bm, d), lambda i, j: (i, 0)),
        pl.BlockSpec((None, 1, d), lambda i, j: (layer, 0, 0)),
        _mod_spec(layer, 1, row_fn, d),
        _mod_spec(layer, 0, row_fn, d),
        pl.BlockSpec((d, dm), lambda i, j: (0, j)),
    ]
    args = [x, g_norm, mod, mod, w_in]
    if rope:
        nblk_seq = seq // bm
        in_specs += [pl.BlockSpec((bm, LANES), lambda i, j: (i % nblk_seq, 0))] * 2
        args += list(tables)
    out_specs = [pl.BlockSpec((bm, dm), lambda i, j: (i, 0))] * 3
    out_shape = [jax.ShapeDtypeStruct((t, dm), BF16)] * 3
    if kv32:
        out_specs.append(pl.BlockSpec((bm, dm), lambda i, j: (i, jnp.clip(j - 1, 0, 1))))
        out_shape.append(jax.ShapeDtypeStruct((t, 2 * dm), F32))
    out_specs.append(pl.BlockSpec((bm, dm), lambda i, j: (i, jnp.clip(j - 3, 0, nz - 1))))
    out_shape.append(jax.ShapeDtypeStruct((t, nz * dm), F32))
    return pl.pallas_call(
        functools.partial(_in_kernel, rope=rope, kv32=kv32),
        grid=(t // bm, ncol // dm),
        in_specs=in_specs,
        out_specs=out_specs,
        out_shape=out_shape,
        scratch_shapes=[pltpu.VMEM((bm, d), BF16)],
        compiler_params=_params("arbitrary", "arbitrary"),
        name="in_projection",
    )(*args)


def _attn_kernel(*refs, has_ctx, lam_init, kc):
    q_ref, k_ref, v_ref = refs[:3]
    pos = 3
    if has_ctx:
        ck_ref, cv_ref = refs[pos:pos + 2]
        pos += 2
    lq1, lk1, lq2, lk2, g_ref, o_ref = refs[pos:pos + 6]

    lam = (jnp.exp(jnp.sum(lq1[...] * lk1[...], axis=-1, keepdims=True))
           - jnp.exp(jnp.sum(lq2[...] * lk2[...], axis=-1, keepdims=True)) + lam_init)

    q = q_ref[...]
    bq = q.shape[0]
    lane = lax.broadcasted_iota(jnp.int32, q.shape, 1)
    zero = jnp.zeros_like(q)
    qs = jnp.concatenate([jnp.where(lane < DH, q, zero), jnp.where(lane >= DH, q, zero)], axis=0)
    nt = (((1,), (1,)), ((), ()))

    chunks = []
    if has_ctx:
        chunks.append((lambda: ck_ref[...].astype(BF16), lambda: cv_ref[...].astype(BF16)))
    for c in range(k_ref.shape[0] // kc):
        chunks.append((lambda c=c: k_ref[c * kc:(c + 1) * kc, :], lambda c=c: v_ref[c * kc:(c + 1) * kc, :]))

    m = l = acc = None
    for load_k, load_v in chunks:
        s = lax.dot_general(qs, load_k(), nt, preferred_element_type=F32)
        smax = jnp.max(s, axis=-1, keepdims=True)
        if m is None:
            m = smax
            e = jnp.exp(s - m)
            l = jnp.sum(e, axis=-1, keepdims=True)
            acc = jnp.dot(e.astype(BF16), load_v(), preferred_element_type=F32)
        else:
            m_new = jnp.maximum(m, smax)
            alpha = jnp.exp(m - m_new)
            e = jnp.exp(s - m_new)
            l = alpha * l + jnp.sum(e, axis=-1, keepdims=True)
            acc = alpha * acc + jnp.dot(e.astype(BF16), load_v(), preferred_element_type=F32)
            m = m_new

    inv = 1.0 / l
    o = acc[:bq] * inv[:bq] - acc[bq:] * (lam * inv[bq:])
    y = o * lax.rsqrt(jnp.mean(o * o, axis=-1, keepdims=True) + EPS) * g_ref[...]
    o_ref[...] = (y * (1.0 - lam_init)).astype(o_ref.dtype)


def _attention(q, k, v, ctx_k, ctx_v, layer, lam_p, g_subln):
    b, n, dm = q.shape
    heads = dm // V_DIM
    bq = _block(n, 256)
    kc = _block(n, KEY_CHUNK)
    has_ctx = ctx_k is not None
    lam_init = 0.8 - 0.6 * math.exp(-0.3 * layer)
    in_specs = [
        pl.BlockSpec((None, bq, V_DIM), lambda bi, h, i: (bi, i, h)),
        pl.BlockSpec((None, n, V_DIM), lambda bi, h, i: (bi, 0, h)),
        pl.BlockSpec((None, n, V_DIM), lambda bi, h, i: (bi, 0, h)),
    ]
    args = [q, k, v]
    if has_ctx:
        past = ctx_k.shape[2]
        in_specs += [pl.BlockSpec((None, None, past, V_DIM), lambda bi, h, i: (bi, layer, 0, h))] * 2
        args += [ctx_k, ctx_v]
    in_specs += [pl.BlockSpec((None, 1, DH), lambda bi, h, i: (layer, 0, 0))] * 4
    in_specs += [pl.BlockSpec((None, 1, V_DIM), lambda bi, h, i: (layer, 0, 0))]
    args += list(lam_p) + [g_subln]
    return pl.pallas_call(
        functools.partial(_attn_kernel, has_ctx=has_ctx, lam_init=lam_init, kc=kc),
        grid=(b, heads, n // bq),
        in_specs=in_specs,
        out_specs=pl.BlockSpec((None, bq, V_DIM), lambda bi, h, i: (bi, i, h)),
        out_shape=jax.ShapeDtypeStruct((b, n, dm), BF16),
        compiler_params=_params("arbitrary", "arbitrary", "arbitrary"),
        name="diff_attention",
    )(*args)


def _halo_specs(bn, halo, width, col, nseq_halo):
    per = bn // halo
    return [
        pl.BlockSpec((None, halo, width), lambda b, i: (b, jnp.maximum(i * per - 1, 0), col)),
        pl.BlockSpec((None, bn, width), lambda b, i: (b, i, col)),
        pl.BlockSpec((None, halo, width), lambda b, i: (b, jnp.minimum((i + 1) * per, nseq_halo - 1), col)),
    ]


def _edge_masks():
    i = pl.program_id(1)
    return i > 0, i < pl.num_programs(1) - 1


def _convmod_kernel(p_ref, m_ref, n_ref, w_ref, b_ref, g_ref, bl_ref, o_ref, *, halo):
    dm = o_ref.shape[-1]
    bn = o_ref.shape[0]
    taps = w_ref.shape[0]
    has_prev, has_next = _edge_masks()

    def glu(ref):
        z = ref[...]
        return z[:, :dm] * _sigmoid(z[:, dm:])

    prev = jnp.where(has_prev, glu(p_ref), 0.0)
    nxt = jnp.where(has_next, glu(n_ref), 0.0)
    win = jnp.concatenate([prev, glu(m_ref), nxt], axis=0)
    w = w_ref[...]
    left = (taps - 1) // 2
    acc = jnp.zeros((bn, dm), F32) + b_ref[...]
    shifted = {}
    for t in range(taps):
        off = halo + t - left
        sub = off % SUBLANES
        if sub not in shifted:
            shifted[sub] = pltpu.roll(win, win.shape[0] - sub, 0) if sub else win
        base = off - sub
        acc = acc + shifted[sub][base:base + bn] * w[t:t + 1]
    mu = jnp.mean(acc, axis=-1, keepdims=True)
    xc = acc - mu
    y = xc * lax.rsqrt(jnp.mean(xc * xc, axis=-1, keepdims=True) + EPS) * g_ref[...] + bl_ref[...]
    o_ref[...] = _silu(y).astype(o_ref.dtype)


def _conv_module(z, layer, w_dw, b_dw, g_ln, b_ln):
    b, n, _ = z.shape
    taps, dm = w_dw.shape[1:]
    halo = 16
    bn = _block(n, 256)
    vec = pl.BlockSpec((None, 1, dm), lambda bi, i: (layer, 0, 0))
    return pl.pallas_call(
        functools.partial(_convmod_kernel, halo=halo),
        grid=(b, n // bn),
        in_specs=_halo_specs(bn, halo, 2 * dm, 0, n // halo)
        + [pl.BlockSpec((None, taps, dm), lambda bi, i: (layer, 0, 0)), vec, vec, vec],
        out_specs=pl.BlockSpec((None, bn, dm), lambda bi, i: (bi, i, 0)),
        out_shape=jax.ShapeDtypeStruct((b, n, dm), BF16),
        compiler_params=_params("arbitrary", "arbitrary"),
        name="conformer_conv",
    )(z, z, z, w_dw, b_dw, g_ln, b_ln)


def _shortconv_kernel(p_ref, m_ref, n_ref, gb_ref, w_ref, o_ref, *, halo):
    dm = o_ref.shape[-1]
    bn = o_ref.shape[0]
    taps = w_ref.shape[0]
    has_prev, has_next = _edge_masks()

    def prod(ref):
        z = ref[...]
        return z[:, :dm] * z[:, dm:]

    prev = jnp.where(has_prev, prod(p_ref), 0.0)
    nxt = jnp.where(has_next, prod(n_ref), 0.0)
    win = jnp.concatenate([prev, prod(m_ref), nxt], axis=0)
    w = w_ref[...]
    left = (taps - 1) // 2
    acc = jnp.zeros((bn, dm), F32)
    for t in range(taps):
        acc = acc + _row_shift(win, halo + t - left, bn) * w[t:t + 1]
    o_ref[...] = (gb_ref[...] * acc).astype(o_ref.dtype)


def _short_conv(z, layer, w_dw):
    b, n, _ = z.shape
    taps, dm = w_dw.shape[1:]
    halo = SUBLANES
    bn = _block(n, 512)
    return pl.pallas_call(
        functools.partial(_shortconv_kernel, halo=halo),
        grid=(b, n // bn),
        in_specs=_halo_specs(bn, halo, 2 * dm, 1, n // halo)
        + [pl.BlockSpec((None, bn, dm), lambda bi, i: (bi, i, 4)),
           pl.BlockSpec((None, taps, dm), lambda bi, i: (layer, 0, 0))],
        out_specs=pl.BlockSpec((None, bn, dm), lambda bi, i: (bi, i, 0)),
        out_shape=jax.ShapeDtypeStruct((b, n, dm), BF16),
        compiler_params=_params("arbitrary", "arbitrary"),
        name="short_conv",
    )(z, z, z, z, w_dw)


def _scan_block(a, u, reverse):
    rows = a.shape[0]
    row = lax.broadcasted_iota(jnp.int32, a.shape, 0)
    s = 1
    while s < rows:
        if reverse:
            a_sh = pltpu.roll(a, rows - s, 0)
            u_sh = pltpu.roll(u, rows - s, 0)
            ok = row < rows - s
        else:
            a_sh = pltpu.roll(a, s, 0)
            u_sh = pltpu.roll(u, s, 0)
            ok = row >= s
        u = jnp.where(ok, a * u_sh + u, u)
        a = jnp.where(ok, a * a_sh, a)
        s *= 2
    return a, u


def _rglru_kernel(x_ref, y_ref, h0_ref, wc_ref, bc_ref, wg_ref, bg_ref, lam_ref,
                  o_ref, hl_ref, xpad, af, uf, ab, ub, *, rows):
    n, cw = x_ref.shape
    nchunk = n // rows
    pad = SUBLANES
    taps = wc_ref.shape[0]
    left = (taps - 1) // 2

    xpad[0:pad, :] = jnp.zeros((pad, cw), F32)
    xpad[pad:pad + n, :] = x_ref[...]
    xpad[pad + n:pad + n + pad, :] = jnp.zeros((pad, cw), F32)

    lam = lam_ref[...]
    softplus = jnp.maximum(-lam, 0.0) + jnp.log1p(jnp.exp(-jnp.abs(lam)))
    wc = wc_ref[...]
    bc = bc_ref[...]
    wg = wg_ref[...]
    bg = bg_ref[...]

    def gates(ci, carry):
        r0 = pl.multiple_of(ci * rows, rows)
        win = xpad[pl.ds(r0, rows + 2 * pad), :]
        xr = jnp.zeros((rows, cw), F32) + bc
        for t in range(taps):
            xr = xr + _row_shift(win, pad + t - left, rows) * wc[t:t + 1]
        g = jnp.dot(xr.astype(BF16), wg, preferred_element_type=F32) + bg
        for d, (a_ref, u_ref) in enumerate(((af, uf), (ab, ub))):
            r = _sigmoid(g[:, (2 * d) * cw:(2 * d + 1) * cw])
            i = _sigmoid(g[:, (2 * d + 1) * cw:(2 * d + 2) * cw])
            a = jnp.exp(-LRU_C * r * softplus[:, d * cw:(d + 1) * cw])
            u = jnp.sqrt(jnp.maximum(1.0 - a * a, 0.0)) * (i * xr)
            a_ref[pl.ds(r0, rows), :] = a
            u_ref[pl.ds(r0, rows), :] = u
        return carry

    lax.fori_loop(0, nchunk, gates, 0)

    def forward(ci, h):
        r0 = pl.multiple_of(ci * rows, rows)
        p, hz = _scan_block(af[pl.ds(r0, rows), :], uf[pl.ds(r0, rows), :], False)
        hs = hz + p * h
        uf[pl.ds(r0, rows), :] = hs
        return hs[rows - 1:rows]

    h_f = lax.fori_loop(0, nchunk, forward, h0_ref[0:1, :])

    def backward(cj, h):
        r0 = pl.multiple_of((nchunk - 1 - cj) * rows, rows)
        p, hz = _scan_block(ab[pl.ds(r0, rows), :], ub[pl.ds(r0, rows), :], True)
        hs = hz + p * h
        yr = y_ref[pl.ds(r0, rows), :]
        o_ref[pl.ds(r0, rows), :] = ((uf[pl.ds(r0, rows), :] + hs) * _gelu_tanh(yr)).astype(o_ref.dtype)
        return hs[0:1]

    h_b = lax.fori_loop(0, nchunk, backward, h0_ref[1:2, :])
    hl_ref[0:1, :] = h_f
    hl_ref[1:2, :] = h_b


def _rglru(z, h0, layer, w_conv4, b_conv4, wg_bd, bg_bd, lam_bd):
    b, n, zc = z.shape
    dm = w_conv4.shape[-1]
    cw = LANES
    nck = dm // cw
    taps = w_conv4.shape[1]
    rows = _block(n, 256)
    xcol = 5 * nck
    ycol = 6 * nck
    seq = pltpu.VMEM((n, cw), F32)
    return pl.pallas_call(
        functools.partial(_rglru_kernel, rows=rows),
        grid=(b, nck),
        in_specs=[
            pl.BlockSpec((None, n, cw), lambda bi, c: (bi, 0, xcol + c)),
            pl.BlockSpec((None, n, cw), lambda bi, c: (bi, 0, ycol + c)),
            pl.BlockSpec((None, 2, cw), lambda bi, c: (bi, 0, c)),
            pl.BlockSpec((None, taps, cw), lambda bi, c: (layer, 0, c)),
            pl.BlockSpec((None, 1, cw), lambda bi, c: (layer, 0, c)),
            pl.BlockSpec((None, None, cw, 4 * cw), lambda bi, c: (layer, c, 0, 0)),
            pl.BlockSpec((None, None, 1, 4 * cw), lambda bi, c: (layer, c, 0, 0)),
            pl.BlockSpec((None, None, 1, 2 * cw), lambda bi, c: (layer, c, 0, 0)),
        ],
        out_specs=[
            pl.BlockSpec((None, n, cw), lambda bi, c: (bi, 0, c)),
            pl.BlockSpec((None, 2, cw), lambda bi, c: (bi, 0, c)),
        ],
        out_shape=[
            jax.ShapeDtypeStruct((b, n, dm), BF16),
            jax.ShapeDtypeStruct((b, 2, dm), F32),
        ],
        scratch_shapes=[pltpu.VMEM((n + 2 * SUBLANES, cw), F32), seq, seq, seq, seq],
        compiler_params=_params("arbitrary", "arbitrary"),
        name="rglru",
    )(z, z, h0, w_conv4, b_conv4, wg_bd, bg_bd, lam_bd)


def _merge_kernel(x_ref, g_ref, sc_ref, sh_ref, ya, yb, yc, yd, g0, g1, g2, g3, b0, b1, b2, b3,
                  o_ref, h_scr):
    @pl.when(pl.program_id(1) == 0)
    def _():
        h_scr[...] = _adaln(x_ref[...], g_ref[...], sc_ref[...], sh_ref[...])

    h = h_scr[...]
    acc = None
    for y_ref, wg_ref, wb_ref in ((ya, g0, b0), (yb, g1, b1), (yc, g2, b2), (yd, g3, b3)):
        gate = _sigmoid(jnp.dot(h, wg_ref[...], preferred_element_type=F32))
        term = gate * jnp.dot(y_ref[...], wb_ref[...], preferred_element_type=F32)
        acc = term if acc is None else acc + term
    o_ref[...] = acc.astype(o_ref.dtype)


def _merge(x, mod, layer, rows_per_mod, base_row, g_norm, ys, w_gate, w_branch):
    t, d = x.shape
    dm = d // 4
    bm, row_fn = _row_blocks(rows_per_mod, base_row)
    bn = _block(d, 256)
    ncb = d // bn
    in_specs = [
        pl.BlockSpec((bm, d), lambda i, j: (i, 0)),
        pl.BlockSpec((None, 1, d), lambda i, j: (layer, 0, 0)),
        _mod_spec(layer, 1, row_fn, d),
        _mod_spec(layer, 0, row_fn, d),
    ]
    in_specs += [pl.BlockSpec((bm, dm), lambda i, j: (i, 0))] * 4
    in_specs += [pl.BlockSpec((d, bn), functools.partial(lambda i, j, jb: (0, jb * ncb + j), jb=jb))
                 for jb in range(4)]
    in_specs += [pl.BlockSpec((None, None, dm, bn), functools.partial(lambda i, j, jb: (layer, jb, 0, j), jb=jb))
                 for jb in range(4)]
    return pl.pallas_call(
        _merge_kernel,
        grid=(t // bm, ncb),
        in_specs=in_specs,
        out_specs=pl.BlockSpec((bm, bn), lambda i, j: (i, j)),
        out_shape=jax.ShapeDtypeStruct((t, d), BF16),
        scratch_shapes=[pltpu.VMEM((bm, d), BF16)],
        compiler_params=_params("arbitrary", "arbitrary"),
        name="gated_merge",
    )(x, g_norm, mod, mod, *ys, *([w_gate] * 4), *([w_branch] * 4))


def _resid_kernel(a_ref, w_ref, x_ref, gt_ref, o_ref):
    o_ref[...] = x_ref[...] + gt_ref[...] * jnp.dot(a_ref[...], w_ref[...], preferred_element_type=F32)


def _residual_proj(a, w, x, mod, layer, which, rows_per_mod, base_row, name):
    t, kdim = a.shape
    d = x.shape[1]
    bm, row_fn = _row_blocks(rows_per_mod, base_row)
    bn = _block(d, 512)
    return pl.pallas_call(
        _resid_kernel,
        grid=(t // bm, d // bn),
        in_specs=[
            pl.BlockSpec((bm, kdim), lambda i, j: (i, 0)),
            pl.BlockSpec((kdim, bn), lambda i, j: (0, j)),
            pl.BlockSpec((bm, bn), lambda i, j: (i, j)),
            _mod_spec(layer, which, row_fn, bn, col=True),
        ],
        out_specs=pl.BlockSpec((bm, bn), lambda i, j: (i, j)),
        out_shape=jax.ShapeDtypeStruct((t, d), F32),
        compiler_params=_params("arbitrary", "arbitrary"),
        name=name,
    )(a, w, x, mod)


def _ffn_up_kernel(p_ref, m_ref, n_ref, g_ref, sc_ref, sh_ref, wa_ref, wu_ref, wc_ref, bc_ref,
                   o_ref, h_scr, *, seq):
    bm = m_ref.shape[0]
    halo = p_ref.shape[0]
    i = pl.program_id(0)

    @pl.when(pl.program_id(1) == 0)
    def _():
        for ref, lo in ((p_ref, 0), (m_ref, halo), (n_ref, halo + bm)):
            h_scr[lo:lo + ref.shape[0], :] = _adaln(ref[...], g_ref[...], sc_ref[...], sh_ref[...])

    h = h_scr[...]
    a = jnp.dot(h, wa_ref[...], preferred_element_type=F32)
    u = jnp.dot(h[halo:halo + bm], wu_ref[...], preferred_element_type=F32)
    tok = i * bm + lax.broadcasted_iota(jnp.int32, u.shape, 0)
    pos = (tok & (seq - 1)) if seq & (seq - 1) == 0 else lax.rem(tok, seq)
    w = wc_ref[...]
    taps = w.shape[0]
    left = (taps - 1) // 2
    acc = jnp.zeros(u.shape, F32) + bc_ref[...]
    for t in range(taps):
        delta = t - left
        tap = _row_shift(a, halo + delta, bm) * w[t:t + 1]
        if delta < 0:
            tap = jnp.where(pos >= -delta, tap, 0.0)
        elif delta > 0:
            tap = jnp.where(pos < seq - delta, tap, 0.0)
        acc = acc + tap
    o_ref[...] = (_silu(acc) * u).astype(o_ref.dtype)


def _ffn_up(x, mod, layer, rows_per_mod, base_row, g_norm, wa, wu, w_conv, b_conv, seq):
    t, d = x.shape
    ffp = wa.shape[-1]
    taps = w_conv.shape[1]
    halo = 16
    bm, row_fn = _row_blocks(rows_per_mod, base_row)
    bn = _block(ffp, 512)
    per = bm // halo
    nhalo = t // halo
    return pl.pallas_call(
        functools.partial(_ffn_up_kernel, seq=seq),
        grid=(t // bm, ffp // bn),
        in_specs=[
            pl.BlockSpec((halo, d), lambda i, j: (jnp.maximum(i * per - 1, 0), 0)),
            pl.BlockSpec((bm, d), lambda i, j: (i, 0)),
            pl.BlockSpec((halo, d), lambda i, j: (jnp.minimum((i + 1) * per, nhalo - 1), 0)),
            pl.BlockSpec((None, 1, d), lambda i, j: (layer, 0, 0)),
            _mod_spec(layer, 4, row_fn, d),
            _mod_spec(layer, 3, row_fn, d),
            pl.BlockSpec((None, d, bn), lambda i, j: (layer, 0, j)),
            pl.BlockSpec((None, d, bn), lambda i, j: (layer, 0, j)),
            pl.BlockSpec((None, taps, bn), lambda i, j: (layer, 0, j)),
            pl.BlockSpec((None, 1, bn), lambda i, j: (layer, 0, j)),
        ],
        out_specs=pl.BlockSpec((bm, bn), lambda i, j: (i, j)),
        out_shape=jax.ShapeDtypeStruct((t, ffp), BF16),
        scratch_shapes=[pltpu.VMEM((bm + 2 * halo, d), BF16)],
        compiler_params=_params("arbitrary", "arbitrary"),
        name="ffn_up",
    )(x, x, x, g_norm, mod, mod, wa, wu, w_conv, b_conv)


def _final_norm_kernel(x_ref, g_ref, o_ref):
    x = x_ref[...]
    o_ref[...] = x * lax.rsqrt(jnp.mean(x * x, axis=-1, keepdims=True) + EPS) * g_ref[...]


def _final_norm(x, g):
    t, d = x.shape
    bm = _block(t, 512)
    return pl.pallas_call(
        _final_norm_kernel,
        grid=(t // bm,),
        in_specs=[pl.BlockSpec((bm, d), lambda i: (i, 0)), pl.BlockSpec((1, d), lambda i: (0, 0))],
        out_specs=pl.BlockSpec((bm, d), lambda i: (i, 0)),
        out_shape=jax.ShapeDtypeStruct((t, d), F32),
        compiler_params=_params("arbitrary"),
        name="final_norm",
    )(x, g.reshape(1, d))


def _rope_tables(n):
    pairs = DH // 4
    rows = n // GRID_W
    t_row = jnp.repeat(jnp.arange(rows), GRID_W).astype(F32)
    t_col = jnp.tile(jnp.arange(GRID_W), rows).astype(F32)
    inv = jnp.power(ROPE_BASE, -jnp.arange(pairs, dtype=F32) / pairs)
    ang_row = t_row[:, None] * inv
    ang_col = t_col[:, None] * inv
    ang = jnp.concatenate([ang_row, ang_row, ang_col, ang_col], axis=1)
    sign = np.tile(np.repeat(np.array([-1.0, 1.0], np.float32), pairs), 2)
    cos = jnp.cos(ang)
    sin = jnp.sin(ang) * sign
    reps = LANES // DH
    return jnp.tile(cos, (1, reps)), jnp.tile(sin, (1, reps))


def _block_diag_gates(w_rg_a, b_rg_a, w_rg_x, b_rg_x, lru_lambda, dm):
    depth = w_rg_a.shape[0]
    cw = LANES
    nck = dm // cw
    per = cw // BS_R

    def dense(w):
        w = w.reshape(depth, 2, nck, per, BS_R, BS_R)
        eye = jnp.eye(per, dtype=w.dtype)
        full = jnp.einsum('ldcpij,pq->ldcpiqj', w, eye)
        return full.reshape(depth, 2, nck, cw, cw)

    wa, wx = dense(w_rg_a), dense(w_rg_x)
    wg = jnp.concatenate([wa[:, 0], wx[:, 0], wa[:, 1], wx[:, 1]], axis=-1).astype(BF16)

    def vec(v):
        return v.reshape(depth, 2, nck, 1, cw)

    ba, bx = vec(b_rg_a), vec(b_rg_x)
    bg = jnp.concatenate([ba[:, 0], bx[:, 0], ba[:, 1], bx[:, 1]], axis=-1)
    lam = vec(lru_lambda)
    lam = jnp.concatenate([lam[:, 0], lam[:, 1]], axis=-1)
    return wg, bg, lam


def kernel(x_prompt, x_sample, cache_k, cache_v, state_lru, c, c_ctx, w_mod, b_mod, g_norm1, g_norm2, g_final, w_in, lam_q1, lam_k1, lam_q2, lam_k2, g_subln, w_dw31, b_dw31, g_ln_conv, b_ln_conv, w_dw3, w_conv4, b_conv4, w_rg_a, b_rg_a, w_rg_x, b_rg_x, lru_lambda, w_branch, w_out, w_ffn_up, w_ffn_conv, b_ffn_conv, w_ffn_down):
    depth, d, _ = w_in.shape
    dm = d // 4
    bc, sc, _ = x_prompt.shape
    bd, sd, _ = x_sample.shape
    past = cache_k.shape[2]
    d_ff = w_ffn_conv.shape[-1]
    ffp = -(-d_ff // 512) * 512

    rows = -(-(bd + 1) // SUBLANES) * SUBLANES
    cvec = jnp.zeros((rows, d), F32).at[:bd].set(c).at[bd].set(c_ctx)
    mod = _modulation(cvec, w_mod, b_mod).reshape(depth, rows, 6, 1, d)

    qkvz = jnp.concatenate([w_in[:, :, :5 * dm], w_in[:, :, 6 * dm:8 * dm], w_in[:, :, 5 * dm:6 * dm],
                            w_in[:, :, 8 * dm:10 * dm]], axis=-1).astype(BF16)
    w_gate = w_in[:, :, 10 * dm:].astype(BF16)
    w_branch_b = w_branch.astype(BF16)
    w_out_b = w_out.astype(BF16)
    padc = [(0, 0), (0, 0), (0, ffp - d_ff)]
    w_up_a = jnp.pad(w_ffn_up[:, :, :d_ff], padc).astype(BF16)
    w_up_u = jnp.pad(w_ffn_up[:, :, d_ff:], padc).astype(BF16)
    w_fc = jnp.pad(w_ffn_conv, padc)
    b_fc = jnp.pad(b_ffn_conv, [(0, 0), (0, ffp - d_ff)]).reshape(depth, 1, ffp)
    w_down = jnp.pad(w_ffn_down, [(0, 0), (0, ffp - d_ff), (0, 0)]).astype(BF16)
    wg_bd, bg_bd, lam_bd = _block_diag_gates(w_rg_a, b_rg_a, w_rg_x, b_rg_x, lru_lambda, dm)

    def v3(p):
        return p.reshape(depth, 1, p.shape[-1])

    g1, g2 = v3(g_norm1), v3(g_norm2)
    lam_p = [v3(lam_q1), v3(lam_k1), v3(lam_q2), v3(lam_k2)]
    g_sub = v3(g_subln)
    b31, gln, bln, b4 = v3(b_dw31), v3(g_ln_conv), v3(b_ln_conv), v3(b_conv4)

    def layer_pass(x, layer, b, n, rpm, base, tables, ctx_k, ctx_v, h0, want_kv):
        outs = _in_projection(x, mod, layer, rpm, base, g1, qkvz[layer], tables, n, want_kv)
        if want_kv:
            q, k, v, kv32, z = outs
        else:
            q, k, v, z = outs
            kv32 = None
        q, k, v, z = (a.reshape(b, n, -1) for a in (q, k, v, z))
        y_a = _attention(q, k, v, ctx_k, ctx_v, layer, lam_p, g_sub)
        y_b = _conv_module(z, layer, w_dw31, b31, gln, bln)
        y_c = _short_conv(z, layer, w_dw3)
        y_d, h_last = _rglru(z, h0, layer, w_conv4, b4, wg_bd, bg_bd, lam_bd)
        ys = [y.reshape(b * n, dm) for y in (y_a, y_b, y_c, y_d)]
        merged = _merge(x, mod, layer, rpm, base, g1, ys, w_gate[layer], w_branch_b)
        x1 = _residual_proj(merged, w_out_b[layer], x, mod, layer, 2, rpm, base, "out_projection")
        act = _ffn_up(x1, mod, layer, rpm, base, g2, w_up_a, w_up_u, w_fc, b_fc, n)
        x2 = _residual_proj(act, w_down[layer], x1, mod, layer, 5, rpm, base, "ffn_down")
        return x2, kv32, h_last

    xp = x_prompt.reshape(bc * sc, d)
    zero_state = jnp.zeros((bc, 2, dm), F32)
    ks, vs, ss = [], [], []
    for layer in range(depth):
        xp, kv32, h_last = layer_pass(xp, layer, bc, sc, bc * sc, bd, None, None, None, zero_state, True)
        kv32 = kv32.reshape(bc, sc, 2 * dm)
        ks.append(kv32[:, :, :dm])
        vs.append(kv32[:, :, dm:])
        ss.append(h_last)
    heads = dm // V_DIM
    y_prompt = _final_norm(xp, g_final).reshape(bc, sc, d)
    new_cache_k = jnp.stack(ks, axis=1).reshape(bc, depth, sc, heads, 2, DH)
    new_cache_v = jnp.stack(vs, axis=1).reshape(bc, depth, sc, heads, V_DIM)
    new_state = jnp.stack(ss, axis=1)

    tables = _rope_tables(sd)
    ck = cache_k.reshape(bd, depth, past, dm)
    cv = cache_v.reshape(bd, depth, past, dm)
    xs = x_sample.reshape(bd * sd, d)
    for layer in range(depth):
        xs, _, _ = layer_pass(xs, layer, bd, sd, sd, 0, tables, ck, cv, state_lru[:, layer], False)
    y_sample = _final_norm(xs, g_final).reshape(bd, sd, d)
    return (y_prompt, y_sample, new_cache_k, new_cache_v, new_state)
```

```python
import functools
import math

import jax
import jax.numpy as jnp
import numpy as np
from jax import lax
from jax.experimental import pallas as pl
from jax.experimental.pallas import tpu as pltpu

F32 = jnp.float32
BF16 = jnp.bfloat16

EPS = 1e-6
ROPE_BASE = 10000.0
GRID_W = 64
LRU_C = 8.0
V_DIM = 128
DH = V_DIM // 2
BS_R = 64
LANES = 128
SUBLANES = 8
VMEM_LIMIT_BYTES = 56 * 1024 * 1024
ROW_BLOCK = 1024
KEY_CHUNK = 1024
QUERY_BLOCK = 256
NORM_CHUNK = 256


def _params(*sem):
    return pltpu.CompilerParams(dimension_semantics=sem, vmem_limit_bytes=VMEM_LIMIT_BYTES)


def _block(n, pref):
    b = min(n, pref)
    while n % b:
        b //= 2
    return b


def _row_blocks(rows_per_mod, base_row, pref=ROW_BLOCK):
    bm = _block(rows_per_mod, pref)
    return bm, (lambda i: base_row + (i * bm) // rows_per_mod)


def _sigmoid(x):
    return 1.0 / (1.0 + jnp.exp(-x))


def _silu(x):
    return x * _sigmoid(x)


def _gelu_tanh(x):
    return 0.5 * x * (1.0 + jnp.tanh(math.sqrt(2.0 / math.pi) * (x + 0.044715 * (x * x * x))))


def _adaln(x, g, sc, sh):
    y = x * lax.rsqrt(jnp.mean(x * x, axis=-1, keepdims=True) + EPS) * g
    return (y * (1.0 + sc) + sh).astype(BF16)


def _row_shift(win, off, rows):
    n = win.shape[0]
    sub = off % SUBLANES
    base = off - sub
    if sub:
        win = pltpu.roll(win, n - sub, 0)
    return win[base:base + rows]


def _mod_kernel(c_ref, w_ref, b_ref, o_ref):
    c = c_ref[...]
    s = _silu(c).astype(BF16)
    o_ref[...] = jnp.dot(s, w_ref[...].astype(BF16), preferred_element_type=F32) + b_ref[...]


def _modulation(cvec, w_mod, b_mod):
    depth, d, n6 = w_mod.shape
    rows = cvec.shape[0]
    bn = _block(n6, 1024)
    return pl.pallas_call(
        _mod_kernel,
        grid=(depth, n6 // bn),
        in_specs=[
            pl.BlockSpec((rows, d), lambda l, n: (0, 0)),
            pl.BlockSpec((None, d, bn), lambda l, n: (l, 0, n)),
            pl.BlockSpec((None, 1, bn), lambda l, n: (l, 0, n)),
        ],
        out_specs=pl.BlockSpec((None, rows, bn), lambda l, n: (l, 0, n)),
        out_shape=jax.ShapeDtypeStruct((depth, rows, n6), F32),
        compiler_params=_params("arbitrary", "arbitrary"),
        name="modulation",
    )(cvec, w_mod, b_mod.reshape(depth, 1, n6))


def _mod_spec(layer, which, row_fn, bn, col=False):
    def full(i, j):
        return (layer, row_fn(i), which, 0, 0)

    def cols(i, j):
        return (layer, row_fn(i), which, 0, j)

    return pl.BlockSpec((None, None, None, 1, bn), cols if col else full)


def _rope_store(out_ref, r0, acc, cos, sin, scale):
    rows = acc.shape[0]
    lane = lax.broadcasted_iota(jnp.int32, (rows, LANES), 1)
    first_half = (lane % 32) < 16
    for c in range(acc.shape[1] // LANES):
        a = acc[:, c * LANES:(c + 1) * LANES]
        up = pltpu.roll(a, LANES - 16, 1)
        down = pltpu.roll(a, 16, 1)
        r = a * cos + jnp.where(first_half, up, down) * sin
        if scale != 1.0:
            r = r * scale
        out_ref[r0:r0 + rows, c * LANES:(c + 1) * LANES] = r.astype(out_ref.dtype)


def _in_kernel(*refs, rope, kv32):
    x_ref, g_ref, sc_ref, sh_ref, w_ref = refs[:5]
    pos = 5
    if rope:
        cos_ref, sin_ref = refs[pos:pos + 2]
        pos += 2
    q_out, k_out, v_out = refs[pos:pos + 3]
    pos += 3
    if kv32:
        kv_out = refs[pos]
        pos += 1
    z_out, h_scr = refs[pos:pos + 2]

    j = pl.program_id(1)
    q_scale = DH ** -0.5

    @pl.when(j == 0)
    def _():
        bm = x_ref.shape[0]
        rc = _block(bm, NORM_CHUNK)
        for r0 in range(0, bm, rc):
            h = _adaln(x_ref[r0:r0 + rc, :], g_ref[...], sc_ref[...], sh_ref[...])
            h_scr[r0:r0 + rc, :] = h
            acc = jnp.dot(h, w_ref[...], preferred_element_type=F32)
            if rope:
                _rope_store(q_out, r0, acc, cos_ref[r0:r0 + rc, :], sin_ref[r0:r0 + rc, :], q_scale)
            else:
                q_out[r0:r0 + rc, :] = (acc * q_scale).astype(BF16)

    @pl.when(j == 1)
    def _():
        bm = x_ref.shape[0]
        rc = _block(bm, NORM_CHUNK) if rope else bm
        for r0 in range(0, bm, rc):
            acc = jnp.dot(h_scr[r0:r0 + rc, :], w_ref[...], preferred_element_type=F32)
            if rope:
                _rope_store(k_out, r0, acc, cos_ref[r0:r0 + rc, :], sin_ref[r0:r0 + rc, :], 1.0)
            else:
                k_out[r0:r0 + rc, :] = acc.astype(BF16)
            if kv32:
                kv_out[r0:r0 + rc, :] = acc

    @pl.when(j > 1)
    def _():
        acc = jnp.dot(h_scr[...], w_ref[...], preferred_element_type=F32)

        @pl.when(j == 2)
        def _():
            v_out[...] = acc.astype(BF16)
            if kv32:
                kv_out[...] = acc

        @pl.when(j >= 3)
        def _():
            z_out[...] = acc


def _in_projection(x, mod, layer, rows_per_mod, base_row, g_norm, w_in, tables, seq, kv32):
    t, d = x.shape
    dm = d // 4
    ncol = w_in.shape[1]
    nz = ncol // dm - 3
    bm, row_fn = _row_blocks(rows_per_mod, base_row)
    rope = tables is not None
    in_specs = [
        pl.BlockSpec((bm, d), lambda i, j: (i, 0)),
        pl.BlockSpec((None, 1, d), lambda i, j: (layer, 0, 0)),
        _mod_spec(layer, 1, row_fn, d),
        _mod_spec(layer, 0, row_fn, d),
        pl.BlockSpec((d, dm), lambda i, j: (0, j)),
    ]
    args = [x, g_norm, mod, mod, w_in]
    if rope:
        nblk_seq = seq // bm
        in_specs += [pl.BlockSpec((bm, LANES), lambda i, j: (i % nblk_seq, 0))] * 2
        args += list(tables)
    out_specs = [pl.BlockSpec((bm, dm), lambda i, j: (i, 0))] * 3
    out_shape = [jax.ShapeDtypeStruct((t, dm), BF16)] * 3
    if kv32:
        out_specs.append(pl.BlockSpec((bm, dm), lambda i, j: (i, jnp.clip(j - 1, 0, 1))))
        out_shape.append(jax.ShapeDtypeStruct((t, 2 * dm), F32))
    out_specs.append(pl.BlockSpec((bm, dm), lambda i, j: (i, jnp.clip(j - 3, 0, nz - 1))))
    out_shape.append(jax.ShapeDtypeStruct((t, nz * dm), F32))
    return pl.pallas_call(
        functools.partial(_in_kernel, rope=rope, kv32=kv32),
        grid=(t // bm, ncol // dm),
        in_specs=in_specs,
        out_specs=out_specs,
        out_shape=out_shape,
        scratch_shapes=[pltpu.VMEM((bm, d), BF16)],
        compiler_params=_params("arbitrary", "arbitrary"),
        name="in_projection",
    )(*args)


def _attn_kernel(*refs, has_ctx, lam_init, kc):
    q_ref, k_ref, vt_ref = refs[:3]
    pos = 3
    if has_ctx:
        ck_ref, cv_ref = refs[pos:pos + 2]
        pos += 2
    lq1, lk1, lq2, lk2, g_ref, o_ref = refs[pos:pos + 6]

    lam = (jnp.exp(jnp.sum(lq1[...] * lk1[...], axis=-1, keepdims=True))
           - jnp.exp(jnp.sum(lq2[...] * lk2[...], axis=-1, keepdims=True)) + lam_init)

    q = q_ref[...]
    bq = q.shape[0]
    lane = lax.broadcasted_iota(jnp.int32, q.shape, 1)
    zero = jnp.zeros_like(q)
    qs = jnp.concatenate([jnp.where(lane < DH, q, zero), jnp.where(lane >= DH, q, zero)], axis=0)
    nt = (((1,), (1,)), ((), ()))

    chunks = []
    if has_ctx:
        chunks.append((lambda: ck_ref[...], lambda: cv_ref[...]))
    for c in range(k_ref.shape[0] // kc):
        chunks.append((lambda c=c: k_ref[c * kc:(c + 1) * kc, :], lambda c=c: vt_ref[:, c * kc:(c + 1) * kc]))

    def scores(ci):
        return lax.dot_general(chunks[ci][0](), qs, nt, preferred_element_type=F32)

    m = l = acc = None
    s_next = scores(0)
    for ci, (_, load_vt) in enumerate(chunks):
        s = s_next
        if ci + 1 < len(chunks):
            s_next = scores(ci + 1)
        smax = jnp.max(s, axis=0, keepdims=True)
        if m is None:
            m = smax
            e = jnp.exp(s - m)
            l = jnp.sum(e, axis=0, keepdims=True)
            acc = jnp.dot(load_vt(), e.astype(BF16), preferred_element_type=F32)
        else:
            m_new = jnp.maximum(m, smax)
            alpha = jnp.exp(m - m_new)
            e = jnp.exp(s - m_new)
            l = alpha * l + jnp.sum(e, axis=0, keepdims=True)
            acc = alpha * acc + jnp.dot(load_vt(), e.astype(BF16), preferred_element_type=F32)
            m = m_new

    inv = 1.0 / l
    o = (acc[:, :bq] * inv[:, :bq] - acc[:, bq:] * (lam * inv[:, bq:])).T
    y = o * lax.rsqrt(jnp.mean(o * o, axis=-1, keepdims=True) + EPS) * g_ref[...]
    o_ref[...] = (y * (1.0 - lam_init)).astype(o_ref.dtype)


def _attention(q, k, v, ctx_k, ctx_vt, layer, lam_p, g_subln):
    b, n, dm = q.shape
    heads = dm // V_DIM
    bq = _block(n, QUERY_BLOCK)
    kc = _block(n, KEY_CHUNK)
    has_ctx = ctx_k is not None
    lam_init = 0.8 - 0.6 * math.exp(-0.3 * layer)
    in_specs = [
        pl.BlockSpec((None, bq, V_DIM), lambda bi, h, i: (bi, i, h)),
        pl.BlockSpec((None, n, V_DIM), lambda bi, h, i: (bi, 0, h)),
        pl.BlockSpec((None, V_DIM, n), lambda bi, h, i: (bi, h, 0)),
    ]
    args = [q, k, jnp.swapaxes(v, 1, 2)]
    if has_ctx:
        past = ctx_k.shape[2]
        in_specs += [pl.BlockSpec((None, None, past, V_DIM), lambda bi, h, i: (bi, layer, 0, h)),
                     pl.BlockSpec((None, None, V_DIM, past), lambda bi, h, i: (bi, layer, h, 0))]
        args += [ctx_k, ctx_vt]
    in_specs += [pl.BlockSpec((None, 1, DH), lambda bi, h, i: (layer, 0, 0))] * 4
    in_specs += [pl.BlockSpec((None, 1, V_DIM), lambda bi, h, i: (layer, 0, 0))]
    args += list(lam_p) + [g_subln]
    return pl.pallas_call(
        functools.partial(_attn_kernel, has_ctx=has_ctx, lam_init=lam_init, kc=kc),
        grid=(b, heads, n // bq),
        in_specs=in_specs,
        out_specs=pl.BlockSpec((None, bq, V_DIM), lambda bi, h, i: (bi, i, h)),
        out_shape=jax.ShapeDtypeStruct((b, n, dm), BF16),
        compiler_params=_params("arbitrary", "arbitrary", "arbitrary"),
        name="diff_attention",
    )(*args)


def _halo_specs(bn, halo, width, col, nseq_halo):
    per = bn // halo
    return [
        pl.BlockSpec((None, halo, width), lambda b, i: (b, jnp.maximum(i * per - 1, 0), col)),
        pl.BlockSpec((None, bn, width), lambda b, i: (b, i, col)),
        pl.BlockSpec((None, halo, width), lambda b, i: (b, jnp.minimum((i + 1) * per, nseq_halo - 1), col)),
    ]


def _edge_masks():
    i = pl.program_id(1)
    return i > 0, i < pl.num_programs(1) - 1


def _convmod_kernel(p_ref, m_ref, n_ref, w_ref, b_ref, g_ref, bl_ref, o_ref, *, halo):
    dm = o_ref.shape[-1]
    bn = o_ref.shape[0]
    taps = w_ref.shape[0]
    has_prev, has_next = _edge_masks()

    def glu(ref):
        z = ref[...]
        return z[:, :dm] * _sigmoid(z[:, dm:])

    prev = jnp.where(has_prev, glu(p_ref), 0.0)
    nxt = jnp.where(has_next, glu(n_ref), 0.0)
    win = jnp.concatenate([prev, glu(m_ref), nxt], axis=0)
    w = w_ref[...]
    left = (taps - 1) // 2
    acc = jnp.zeros((bn, dm), F32) + b_ref[...]
    shifted = {}
    for t in range(taps):
        off = halo + t - left
        sub = off % SUBLANES
        if sub not in shifted:
            shifted[sub] = pltpu.roll(win, win.shape[0] - sub, 0) if sub else win
        base = off - sub
        acc = acc + shifted[sub][base:base + bn] * w[t:t + 1]
    mu = jnp.mean(acc, axis=-1, keepdims=True)
    xc = acc - mu
    y = xc * lax.rsqrt(jnp.mean(xc * xc, axis=-1, keepdims=True) + EPS) * g_ref[...] + bl_ref[...]
    o_ref[...] = _silu(y).astype(o_ref.dtype)


def _conv_module(z, layer, w_dw, b_dw, g_ln, b_ln):
    b, n, _ = z.shape
    taps, dm = w_dw.shape[1:]
    halo = 16
    bn = _block(n, 256)
    vec = pl.BlockSpec((None, 1, dm), lambda bi, i: (layer, 0, 0))
    return pl.pallas_call(
        functools.partial(_convmod_kernel, halo=halo),
        grid=(b, n // bn),
        in_specs=_halo_specs(bn, halo, 2 * dm, 0, n // halo)
        + [pl.BlockSpec((None, taps, dm), lambda bi, i: (layer, 0, 0)), vec, vec, vec],
        out_specs=pl.BlockSpec((None, bn, dm), lambda bi, i: (bi, i, 0)),
        out_shape=jax.ShapeDtypeStruct((b, n, dm), BF16),
        compiler_params=_params("arbitrary", "arbitrary"),
        name="conformer_conv",
    )(z, z, z, w_dw, b_dw, g_ln, b_ln)


def _shortconv_kernel(p_ref, m_ref, n_ref, gb_ref, w_ref, o_ref, *, halo):
    dm = o_ref.shape[-1]
    bn = o_ref.shape[0]
    taps = w_ref.shape[0]
    has_prev, has_next = _edge_masks()

    def prod(ref):
        z = ref[...]
        return z[:, :dm] * z[:, dm:]

    prev = jnp.where(has_prev, prod(p_ref), 0.0)
    nxt = jnp.where(has_next, prod(n_ref), 0.0)
    win = jnp.concatenate([prev, prod(m_ref), nxt], axis=0)
    w = w_ref[...]
    left = (taps - 1) // 2
    acc = jnp.zeros((bn, dm), F32)
    for t in range(taps):
        acc = acc + _row_shift(win, halo + t - left, bn) * w[t:t + 1]
    o_ref[...] = (gb_ref[...] * acc).astype(o_ref.dtype)


def _short_conv(z, layer, w_dw):
    b, n, _ = z.shape
    taps, dm = w_dw.shape[1:]
    halo = SUBLANES
    bn = _block(n, 512)
    return pl.pallas_call(
        functools.partial(_shortconv_kernel, halo=halo),
        grid=(b, n // bn),
        in_specs=_halo_specs(bn, halo, 2 * dm, 1, n // halo)
        + [pl.BlockSpec((None, bn, dm), lambda bi, i: (bi, i, 4)),
           pl.BlockSpec((None, taps, dm), lambda bi, i: (layer, 0, 0))],
        out_specs=pl.BlockSpec((None, bn, dm), lambda bi, i: (bi, i, 0)),
        out_shape=jax.ShapeDtypeStruct((b, n, dm), BF16),
        compiler_params=_params("arbitrary", "arbitrary"),
        name="short_conv",
    )(z, z, z, z, w_dw)


def _scan_block(a, u, reverse):
    rows = a.shape[0]
    row = lax.broadcasted_iota(jnp.int32, a.shape, 0)
    s = 1
    while s < rows:
        if reverse:
            a_sh = pltpu.roll(a, rows - s, 0)
            u_sh = pltpu.roll(u, rows - s, 0)
            ok = row < rows - s
        else:
            a_sh = pltpu.roll(a, s, 0)
            u_sh = pltpu.roll(u, s, 0)
            ok = row >= s
        u = jnp.where(ok, a * u_sh + u, u)
        a = jnp.where(ok, a * a_sh, a)
        s *= 2
    return a, u


def _rglru_kernel(x_ref, y_ref, h0_ref, wc_ref, bc_ref, wg_ref, bg_ref, lam_ref,
                  o_ref, hl_ref, xpad, af, uf, ab, ub, *, rows):
    n, cw = x_ref.shape
    nchunk = n // rows
    pad = SUBLANES
    taps = wc_ref.shape[0]
    left = (taps - 1) // 2

    xpad[0:pad, :] = jnp.zeros((pad, cw), F32)
    xpad[pad:pad + n, :] = x_ref[...]
    xpad[pad + n:pad + n + pad, :] = jnp.zeros((pad, cw), F32)

    lam = lam_ref[...]
    softplus = jnp.maximum(-lam, 0.0) + jnp.log1p(jnp.exp(-jnp.abs(lam)))
    wc = wc_ref[...]
    bc = bc_ref[...]
    wg = wg_ref[...]
    bg = bg_ref[...]

    def gates(ci, carry):
        r0 = pl.multiple_of(ci * rows, rows)
        win = xpad[pl.ds(r0, rows + 2 * pad), :]
        xr = jnp.zeros((rows, cw), F32) + bc
        for t in range(taps):
            xr = xr + _row_shift(win, pad + t - left, rows) * wc[t:t + 1]
        g = jnp.dot(xr.astype(BF16), wg, preferred_element_type=F32) + bg
        for d, (a_ref, u_ref) in enumerate(((af, uf), (ab, ub))):
            r = _sigmoid(g[:, (2 * d) * cw:(2 * d + 1) * cw])
            i = _sigmoid(g[:, (2 * d + 1) * cw:(2 * d + 2) * cw])
            a = jnp.exp(-LRU_C * r * softplus[:, d * cw:(d + 1) * cw])
            u = jnp.sqrt(jnp.maximum(1.0 - a * a, 0.0)) * (i * xr)
            a_ref[pl.ds(r0, rows), :] = a
            u_ref[pl.ds(r0, rows), :] = u
        return carry

    lax.fori_loop(0, nchunk, gates, 0)

    def forward(ci, h):
        r0 = pl.multiple_of(ci * rows, rows)
        p, hz = _scan_block(af[pl.ds(r0, rows), :], uf[pl.ds(r0, rows), :], False)
        hs = hz + p * h
        uf[pl.ds(r0, rows), :] = hs
        return hs[rows - 1:rows]

    h_f = lax.fori_loop(0, nchunk, forward, h0_ref[0:1, :])

    def backward(cj, h):
        r0 = pl.multiple_of((nchunk - 1 - cj) * rows, rows)
        p, hz = _scan_block(ab[pl.ds(r0, rows), :], ub[pl.ds(r0, rows), :], True)
        hs = hz + p * h
        yr = y_ref[pl.ds(r0, rows), :]
        o_ref[pl.ds(r0, rows), :] = ((uf[pl.ds(r0, rows), :] + hs) * _gelu_tanh(yr)).astype(o_ref.dtype)
        return hs[0:1]

    h_b = lax.fori_loop(0, nchunk, backward, h0_ref[1:2, :])
    hl_ref[0:1, :] = h_f
    hl_ref[1:2, :] = h_b


def _rglru(z, h0, layer, w_conv4, b_conv4, wg_bd, bg_bd, lam_bd):
    b, n, zc = z.shape
    dm = w_conv4.shape[-1]
    cw = LANES
    nck = dm // cw
    taps = w_conv4.shape[1]
    rows = _block(n, 256)
    xcol = 5 * nck
    ycol = 6 * nck
    seq = pltpu.VMEM((n, cw), F32)
    return pl.pallas_call(
        functools.partial(_rglru_kernel, rows=rows),
        grid=(b, nck),
        in_specs=[
            pl.BlockSpec((None, n, cw), lambda bi, c: (bi, 0, xcol + c)),
            pl.BlockSpec((None, n, cw), lambda bi, c: (bi, 0, ycol + c)),
            pl.BlockSpec((None, 2, cw), lambda bi, c: (bi, 0, c)),
            pl.BlockSpec((None, taps, cw), lambda bi, c: (layer, 0, c)),
            pl.BlockSpec((None, 1, cw), lambda bi, c: (layer, 0, c)),
            pl.BlockSpec((None, None, cw, 4 * cw), lambda bi, c: (layer, c, 0, 0)),
            pl.BlockSpec((None, None, 1, 4 * cw), lambda bi, c: (layer, c, 0, 0)),
            pl.BlockSpec((None, None, 1, 2 * cw), lambda bi, c: (layer, c, 0, 0)),
        ],
        out_specs=[
            pl.BlockSpec((None, n, cw), lambda bi, c: (bi, 0, c)),
            pl.BlockSpec((None, 2, cw), lambda bi, c: (bi, 0, c)),
        ],
        out_shape=[
            jax.ShapeDtypeStruct((b, n, dm), BF16),
            jax.ShapeDtypeStruct((b, 2, dm), F32),
        ],
        scratch_shapes=[pltpu.VMEM((n + 2 * SUBLANES, cw), F32), seq, seq, seq, seq],
        compiler_params=_params("arbitrary", "arbitrary"),
        name="rglru",
    )(z, z, h0, w_conv4, b_conv4, wg_bd, bg_bd, lam_bd)


def _merge_kernel(x_ref, g_ref, sc_ref, sh_ref, ya, yb, yc, yd, g0, g1, g2, g3, b0, b1, b2, b3,
                  o_ref, h_scr):
    bm = x_ref.shape[0]
    j = pl.program_id(1)

    def merge_rows(h, r0, rows):
        acc = None
        for y_ref, wg_ref, wb_ref in ((ya, g0, b0), (yb, g1, b1), (yc, g2, b2), (yd, g3, b3)):
            gate = _sigmoid(jnp.dot(h, wg_ref[...], preferred_element_type=F32))
            term = gate * jnp.dot(y_ref[r0:r0 + rows, :], wb_ref[...], preferred_element_type=F32)
            acc = term if acc is None else acc + term
        o_ref[r0:r0 + rows, :] = acc.astype(o_ref.dtype)

    @pl.when(j == 0)
    def _():
        rc = _block(bm, NORM_CHUNK)
        for r0 in range(0, bm, rc):
            h = _adaln(x_ref[r0:r0 + rc, :], g_ref[...], sc_ref[...], sh_ref[...])
            h_scr[r0:r0 + rc, :] = h
            merge_rows(h, r0, rc)

    @pl.when(j > 0)
    def _():
        merge_rows(h_scr[...], 0, bm)


def _merge(x, mod, layer, rows_per_mod, base_row, g_norm, ys, w_gate, w_branch):
    t, d = x.shape
    dm = d // 4
    bm, row_fn = _row_blocks(rows_per_mod, base_row)
    bn = _block(d, 256)
    ncb = d // bn
    in_specs = [
        pl.BlockSpec((bm, d), lambda i, j: (i, 0)),
        pl.BlockSpec((None, 1, d), lambda i, j: (layer, 0, 0)),
        _mod_spec(layer, 1, row_fn, d),
        _mod_spec(layer, 0, row_fn, d),
    ]
    in_specs += [pl.BlockSpec((bm, dm), lambda i, j: (i, 0))] * 4
    in_specs += [pl.BlockSpec((d, bn), functools.partial(lambda i, j, jb: (0, jb * ncb + j), jb=jb))
                 for jb in range(4)]
    in_specs += [pl.BlockSpec((None, None, dm, bn), functools.partial(lambda i, j, jb: (layer, jb, 0, j), jb=jb))
                 for jb in range(4)]
    return pl.pallas_call(
        _merge_kernel,
        grid=(t // bm, ncb),
        in_specs=in_specs,
        out_specs=pl.BlockSpec((bm, bn), lambda i, j: (i, j)),
        out_shape=jax.ShapeDtypeStruct((t, d), BF16),
        scratch_shapes=[pltpu.VMEM((bm, d), BF16)],
        compiler_params=_params("arbitrary", "arbitrary"),
        name="gated_merge",
    )(x, g_norm, mod, mod, *ys, *([w_gate] * 4), *([w_branch] * 4))


def _resid_kernel(a_ref, w_ref, x_ref, gt_ref, o_ref):
    o_ref[...] = x_ref[...] + gt_ref[...] * jnp.dot(a_ref[...], w_ref[...], preferred_element_type=F32)


def _residual_proj(a, w, x, mod, layer, which, rows_per_mod, base_row, name):
    t, kdim = a.shape
    d = x.shape[1]
    bm, row_fn = _row_blocks(rows_per_mod, base_row)
    bn = _block(d, 512)
    return pl.pallas_call(
        _resid_kernel,
        grid=(t // bm, d // bn),
        in_specs=[
            pl.BlockSpec((bm, kdim), lambda i, j: (i, 0)),
            pl.BlockSpec((kdim, bn), lambda i, j: (0, j)),
            pl.BlockSpec((bm, bn), lambda i, j: (i, j)),
            _mod_spec(layer, which, row_fn, bn, col=True),
        ],
        out_specs=pl.BlockSpec((bm, bn), lambda i, j: (i, j)),
        out_shape=jax.ShapeDtypeStruct((t, d), F32),
        compiler_params=_params("arbitrary", "arbitrary"),
        name=name,
    )(a, w, x, mod)


def _ffn_up_kernel(p_ref, m_ref, n_ref, g_ref, sc_ref, sh_ref, wa_ref, wu_ref, wc_ref, bc_ref,
                   o_ref, h_scr, *, seq):
    bm = m_ref.shape[0]
    halo = p_ref.shape[0]
    i = pl.program_id(0)
    j = pl.program_id(1)

    def norm_rows(ref, r0, rows, lo):
        h_scr[lo:lo + rows, :] = _adaln(ref[r0:r0 + rows, :], g_ref[...], sc_ref[...], sh_ref[...])

    def ffn_rows(r0, rows):
        h = h_scr[r0:r0 + rows + 2 * halo, :]
        a = jnp.dot(h, wa_ref[...], preferred_element_type=F32)
        u = jnp.dot(h[halo:halo + rows], wu_ref[...], preferred_element_type=F32)
        tok = i * bm + r0 + lax.broadcasted_iota(jnp.int32, u.shape, 0)
        pos = (tok & (seq - 1)) if seq & (seq - 1) == 0 else lax.rem(tok, seq)
        w = wc_ref[...]
        taps = w.shape[0]
        left = (taps - 1) // 2
        acc = jnp.zeros(u.shape, F32) + bc_ref[...]
        for t in range(taps):
            delta = t - left
            tap = _row_shift(a, halo + delta, rows) * w[t:t + 1]
            if delta < 0:
                tap = jnp.where(pos >= -delta, tap, 0.0)
            elif delta > 0:
                tap = jnp.where(pos < seq - delta, tap, 0.0)
            acc = acc + tap
        o_ref[r0:r0 + rows, :] = (_silu(acc) * u).astype(o_ref.dtype)

    @pl.when(j == 0)
    def _():
        rc = _block(bm, NORM_CHUNK)
        norm_rows(p_ref, 0, halo, 0)
        norm_rows(m_ref, 0, rc, halo)
        for r0 in range(0, bm, rc):
            if r0 + rc < bm:
                norm_rows(m_ref, r0 + rc, rc, halo + r0 + rc)
            else:
                norm_rows(n_ref, 0, halo, halo + bm)
            ffn_rows(r0, rc)

    @pl.when(j > 0)
    def _():
        ffn_rows(0, bm)


def _ffn_up(x, mod, layer, rows_per_mod, base_row, g_norm, wa, wu, w_conv, b_conv, seq):
    t, d = x.shape
    ffp = wa.shape[-1]
    taps = w_conv.shape[1]
    halo = 16
    bm, row_fn = _row_blocks(rows_per_mod, base_row)
    bn = _block(ffp, 512)
    per = bm // halo
    nhalo = t // halo
    return pl.pallas_call(
        functools.partial(_ffn_up_kernel, seq=seq),
        grid=(t // bm, ffp // bn),
        in_specs=[
            pl.BlockSpec((halo, d), lambda i, j: (jnp.maximum(i * per - 1, 0), 0)),
            pl.BlockSpec((bm, d), lambda i, j: (i, 0)),
            pl.BlockSpec((halo, d), lambda i, j: (jnp.minimum((i + 1) * per, nhalo - 1), 0)),
            pl.BlockSpec((None, 1, d), lambda i, j: (layer, 0, 0)),
            _mod_spec(layer, 4, row_fn, d),
            _mod_spec(layer, 3, row_fn, d),
            pl.BlockSpec((None, d, bn), lambda i, j: (layer, 0, j)),
            pl.BlockSpec((None, d, bn), lambda i, j: (layer, 0, j)),
            pl.BlockSpec((None, taps, bn), lambda i, j: (layer, 0, j)),
            pl.BlockSpec((None, 1, bn), lambda i, j: (layer, 0, j)),
        ],
        out_specs=pl.BlockSpec((bm, bn), lambda i, j: (i, j)),
        out_shape=jax.ShapeDtypeStruct((t, ffp), BF16),
        scratch_shapes=[pltpu.VMEM((bm + 2 * halo, d), BF16)],
        compiler_params=_params("arbitrary", "arbitrary"),
        name="ffn_up",
    )(x, x, x, g_norm, mod, mod, wa, wu, w_conv, b_conv)


def _final_norm_kernel(x_ref, g_ref, o_ref):
    x = x_ref[...]
    o_ref[...] = x * lax.rsqrt(jnp.mean(x * x, axis=-1, keepdims=True) + EPS) * g_ref[...]


def _final_norm(x, g):
    t, d = x.shape
    bm = _block(t, 512)
    return pl.pallas_call(
        _final_norm_kernel,
        grid=(t // bm,),
        in_specs=[pl.BlockSpec((bm, d), lambda i: (i, 0)), pl.BlockSpec((1, d), lambda i: (0, 0))],
        out_specs=pl.BlockSpec((bm, d), lambda i: (i, 0)),
        out_shape=jax.ShapeDtypeStruct((t, d), F32),
        compiler_params=_params("arbitrary"),
        name="final_norm",
    )(x, g.reshape(1, d))


def _rope_tables(n):
    pairs = DH // 4
    rows = n // GRID_W
    t_row = jnp.repeat(jnp.arange(rows), GRID_W).astype(F32)
    t_col = jnp.tile(jnp.arange(GRID_W), rows).astype(F32)
    inv = jnp.power(ROPE_BASE, -jnp.arange(pairs, dtype=F32) / pairs)
    ang_row = t_row[:, None] * inv
    ang_col = t_col[:, None] * inv
    ang = jnp.concatenate([ang_row, ang_row, ang_col, ang_col], axis=1)
    sign = np.tile(np.repeat(np.array([-1.0, 1.0], np.float32), pairs), 2)
    cos = jnp.cos(ang)
    sin = jnp.sin(ang) * sign
    reps = LANES // DH
    return jnp.tile(cos, (1, reps)), jnp.tile(sin, (1, reps))


def _block_diag_gates(w_rg_a, b_rg_a, w_rg_x, b_rg_x, lru_lambda, dm):
    depth = w_rg_a.shape[0]
    cw = LANES
    nck = dm // cw
    per = cw // BS_R

    def dense(w):
        w = w.reshape(depth, 2, nck, per, BS_R, BS_R)
        eye = jnp.eye(per, dtype=w.dtype)
        full = jnp.einsum('ldcpij,pq->ldcpiqj', w, eye)
        return full.reshape(depth, 2, nck, cw, cw)

    wa, wx = dense(w_rg_a), dense(w_rg_x)
    wg = jnp.concatenate([wa[:, 0], wx[:, 0], wa[:, 1], wx[:, 1]], axis=-1).astype(BF16)

    def vec(v):
        return v.reshape(depth, 2, nck, 1, cw)

    ba, bx = vec(b_rg_a), vec(b_rg_x)
    bg = jnp.concatenate([ba[:, 0], bx[:, 0], ba[:, 1], bx[:, 1]], axis=-1)
    lam = vec(lru_lambda)
    lam = jnp.concatenate([lam[:, 0], lam[:, 1]], axis=-1)
    return wg, bg, lam


def kernel(x_prompt, x_sample, cache_k, cache_v, state_lru, c, c_ctx, w_mod, b_mod, g_norm1, g_norm2, g_final, w_in, lam_q1, lam_k1, lam_q2, lam_k2, g_subln, w_dw31, b_dw31, g_ln_conv, b_ln_conv, w_dw3, w_conv4, b_conv4, w_rg_a, b_rg_a, w_rg_x, b_rg_x, lru_lambda, w_branch, w_out, w_ffn_up, w_ffn_conv, b_ffn_conv, w_ffn_down):
    depth, d, _ = w_in.shape
    dm = d // 4
    bc, sc, _ = x_prompt.shape
    bd, sd, _ = x_sample.shape
    past = cache_k.shape[2]
    d_ff = w_ffn_conv.shape[-1]
    ffp = -(-d_ff // 512) * 512

    rows = -(-(bd + 1) // SUBLANES) * SUBLANES
    cvec = jnp.zeros((rows, d), F32).at[:bd].set(c).at[bd].set(c_ctx)
    mod = _modulation(cvec, w_mod, b_mod).reshape(depth, rows, 6, 1, d)

    qkvz = jnp.concatenate([w_in[:, :, :5 * dm], w_in[:, :, 6 * dm:8 * dm], w_in[:, :, 5 * dm:6 * dm],
                            w_in[:, :, 8 * dm:10 * dm]], axis=-1).astype(BF16)
    w_gate = w_in[:, :, 10 * dm:].astype(BF16)
    w_branch_b = w_branch.astype(BF16)
    w_out_b = w_out.astype(BF16)
    padc = [(0, 0), (0, 0), (0, ffp - d_ff)]
    w_up_a = jnp.pad(w_ffn_up[:, :, :d_ff], padc).astype(BF16)
    w_up_u = jnp.pad(w_ffn_up[:, :, d_ff:], padc).astype(BF16)
    w_fc = jnp.pad(w_ffn_conv, padc)
    b_fc = jnp.pad(b_ffn_conv, [(0, 0), (0, ffp - d_ff)]).reshape(depth, 1, ffp)
    w_down = jnp.pad(w_ffn_down, [(0, 0), (0, ffp - d_ff), (0, 0)]).astype(BF16)
    wg_bd, bg_bd, lam_bd = _block_diag_gates(w_rg_a, b_rg_a, w_rg_x, b_rg_x, lru_lambda, dm)

    def v3(p):
        return p.reshape(depth, 1, p.shape[-1])

    g1, g2 = v3(g_norm1), v3(g_norm2)
    lam_p = [v3(lam_q1), v3(lam_k1), v3(lam_q2), v3(lam_k2)]
    g_sub = v3(g_subln)
    b31, gln, bln, b4 = v3(b_dw31), v3(g_ln_conv), v3(b_ln_conv), v3(b_conv4)

    def layer_pass(x, layer, b, n, rpm, base, tables, ctx_k, ctx_v, h0, want_kv):
        outs = _in_projection(x, mod, layer, rpm, base, g1, qkvz[layer], tables, n, want_kv)
        if want_kv:
            q, k, v, kv32, z = outs
        else:
            q, k, v, z = outs
            kv32 = None
        q, k, v, z = (a.reshape(b, n, -1) for a in (q, k, v, z))
        y_a = _attention(q, k, v, ctx_k, ctx_v, layer, lam_p, g_sub)
        y_b = _conv_module(z, layer, w_dw31, b31, gln, bln)
        y_c = _short_conv(z, layer, w_dw3)
        y_d, h_last = _rglru(z, h0, layer, w_conv4, b4, wg_bd, bg_bd, lam_bd)
        ys = [y.reshape(b * n, dm) for y in (y_a, y_b, y_c, y_d)]
        merged = _merge(x, mod, layer, rpm, base, g1, ys, w_gate[layer], w_branch_b)
        x1 = _residual_proj(merged, w_out_b[layer], x, mod, layer, 2, rpm, base, "out_projection")
        act = _ffn_up(x1, mod, layer, rpm, base, g2, w_up_a, w_up_u, w_fc, b_fc, n)
        x2 = _residual_proj(act, w_down[layer], x1, mod, layer, 5, rpm, base, "ffn_down")
        return x2, kv32, h_last

    xp = x_prompt.reshape(bc * sc, d)
    zero_state = jnp.zeros((bc, 2, dm), F32)
    ks, vs, ss = [], [], []
    for layer in range(depth):
        xp, kv32, h_last = layer_pass(xp, layer, bc, sc, bc * sc, bd, None, None, None, zero_state, True)
        kv32 = kv32.reshape(bc, sc, 2 * dm)
        ks.append(kv32[:, :, :dm])
        vs.append(kv32[:, :, dm:])
        ss.append(h_last)
    heads = dm // V_DIM
    y_prompt = _final_norm(xp, g_final).reshape(bc, sc, d)
    new_cache_k = jnp.stack(ks, axis=1).reshape(bc, depth, sc, heads, 2, DH)
    new_cache_v = jnp.stack(vs, axis=1).reshape(bc, depth, sc, heads, V_DIM)
    new_state = jnp.stack(ss, axis=1)

    tables = _rope_tables(sd)
    ck = cache_k.reshape(bd, depth, past, dm).astype(BF16)
    cv = jnp.swapaxes(cache_v.reshape(bd, depth, past, dm), 2, 3).astype(BF16)
    xs = x_sample.reshape(bd * sd, d)
    for layer in range(depth):
        xs, _, _ = layer_pass(xs, layer, bd, sd, sd, 0, tables, ck, cv, state_lru[:, layer], False)
    y_sample = _final_norm(xs, g_final).reshape(bd, sd, d)
    return (y_prompt, y_sample, new_cache_k, new_cache_v, new_state)
```

```python
import functools
import math

import jax
import jax.numpy as jnp
import numpy as np
from jax import lax
from jax.experimental import pallas as pl
from jax.experimental.pallas import tpu as pltpu

F32 = jnp.float32
BF16 = jnp.bfloat16

EPS = 1e-6
ROPE_BASE = 10000.0
GRID_W = 64
LRU_C = 8.0
V_DIM = 128
DH = V_DIM // 2
V_ROWS = V_DIM + 16
BS_R = 64
LANES = 128
SUBLANES = 8
VMEM_LIMIT_BYTES = 56 * 1024 * 1024
ROW_BLOCK = 1024
KEY_CHUNK = 1024
QUERY_BLOCK = 256
HEADS_PER_STEP = 2
EXP_STRIP = 32
NORM_CHUNK = 256


def _params(*sem):
    return pltpu.CompilerParams(dimension_semantics=sem, vmem_limit_bytes=VMEM_LIMIT_BYTES)


def _block(n, pref):
    b = min(n, pref)
    while n % b:
        b //= 2
    return b


def _row_blocks(rows_per_mod, base_row, pref=ROW_BLOCK):
    bm = _block(rows_per_mod, pref)
    return bm, (lambda i: base_row + (i * bm) // rows_per_mod)


def _sigmoid(x):
    return 1.0 / (1.0 + jnp.exp(-x))


def _silu(x):
    return x * _sigmoid(x)


def _gelu_tanh(x):
    return 0.5 * x * (1.0 + jnp.tanh(math.sqrt(2.0 / math.pi) * (x + 0.044715 * (x * x * x))))


def _adaln(x, g, sc, sh):
    y = x * lax.rsqrt(jnp.mean(x * x, axis=-1, keepdims=True) + EPS) * g
    return (y * (1.0 + sc) + sh).astype(BF16)


def _row_shift(win, off, rows):
    n = win.shape[0]
    sub = off % SUBLANES
    base = off - sub
    if sub:
        win = pltpu.roll(win, n - sub, 0)
    return win[base:base + rows]


def _mod_kernel(c_ref, w_ref, b_ref, o_ref):
    c = c_ref[...]
    s = _silu(c).astype(BF16)
    o_ref[...] = jnp.dot(s, w_ref[...].astype(BF16), preferred_element_type=F32) + b_ref[...]


def _modulation(cvec, w_mod, b_mod):
    depth, d, n6 = w_mod.shape
    rows = cvec.shape[0]
    bn = _block(n6, 1024)
    return pl.pallas_call(
        _mod_kernel,
        grid=(depth, n6 // bn),
        in_specs=[
            pl.BlockSpec((rows, d), lambda l, n: (0, 0)),
            pl.BlockSpec((None, d, bn), lambda l, n: (l, 0, n)),
            pl.BlockSpec((None, 1, bn), lambda l, n: (l, 0, n)),
        ],
        out_specs=pl.BlockSpec((None, rows, bn), lambda l, n: (l, 0, n)),
        out_shape=jax.ShapeDtypeStruct((depth, rows, n6), F32),
        compiler_params=_params("arbitrary", "arbitrary"),
        name="modulation",
    )(cvec, w_mod, b_mod.reshape(depth, 1, n6))


def _mod_spec(layer, which, row_fn, bn, col=False):
    def full(i, j):
        return (layer, row_fn(i), which, 0, 0)

    def cols(i, j):
        return (layer, row_fn(i), which, 0, j)

    return pl.BlockSpec((None, None, None, 1, bn), cols if col else full)


def _rope_store(out_ref, r0, acc, cos, sin, scale):
    rows = acc.shape[0]
    lane = lax.broadcasted_iota(jnp.int32, (rows, LANES), 1)
    first_half = (lane % 32) < 16
    for c in range(acc.shape[1] // LANES):
        a = acc[:, c * LANES:(c + 1) * LANES]
        up = pltpu.roll(a, LANES - 16, 1)
        down = pltpu.roll(a, 16, 1)
        r = a * cos + jnp.where(first_half, up, down) * sin
        if scale != 1.0:
            r = r * scale
        out_ref[r0:r0 + rows, c * LANES:(c + 1) * LANES] = r.astype(out_ref.dtype)


def _in_kernel(*refs, rope, kv32):
    x_ref, g_ref, sc_ref, sh_ref, w_ref = refs[:5]
    pos = 5
    if rope:
        cos_ref, sin_ref = refs[pos:pos + 2]
        pos += 2
    q_out, k_out, v_out = refs[pos:pos + 3]
    pos += 3
    if kv32:
        kv_out = refs[pos]
        pos += 1
    z_out, h_scr = refs[pos:pos + 2]

    j = pl.program_id(1)
    q_scale = DH ** -0.5 * math.log2(math.e)

    @pl.when(j == 0)
    def _():
        bm = x_ref.shape[0]
        rc = _block(bm, NORM_CHUNK)
        for r0 in range(0, bm, rc):
            h = _adaln(x_ref[r0:r0 + rc, :], g_ref[...], sc_ref[...], sh_ref[...])
            h_scr[r0:r0 + rc, :] = h
            acc = jnp.dot(h, w_ref[...], preferred_element_type=F32)
            if rope:
                _rope_store(q_out, r0, acc, cos_ref[r0:r0 + rc, :], sin_ref[r0:r0 + rc, :], q_scale)
            else:
                q_out[r0:r0 + rc, :] = (acc * q_scale).astype(BF16)

    @pl.when(j == 1)
    def _():
        bm = x_ref.shape[0]
        rc = _block(bm, NORM_CHUNK) if rope else bm
        for r0 in range(0, bm, rc):
            acc = jnp.dot(h_scr[r0:r0 + rc, :], w_ref[...], preferred_element_type=F32)
            if rope:
                _rope_store(k_out, r0, acc, cos_ref[r0:r0 + rc, :], sin_ref[r0:r0 + rc, :], 1.0)
            else:
                k_out[r0:r0 + rc, :] = acc.astype(BF16)
            if kv32:
                kv_out[r0:r0 + rc, :] = acc

    @pl.when(j > 1)
    def _():
        acc = jnp.dot(h_scr[...], w_ref[...], preferred_element_type=F32)

        @pl.when(j == 2)
        def _():
            v_out[...] = acc.astype(BF16)
            if kv32:
                kv_out[...] = acc

        @pl.when(j >= 3)
        def _():
            z_out[...] = acc


def _in_projection(x, mod, layer, rows_per_mod, base_row, g_norm, w_in, tables, seq, kv32):
    t, d = x.shape
    dm = d // 4
    ncol = w_in.shape[1]
    nz = ncol // dm - 3
    bm, row_fn = _row_blocks(rows_per_mod, base_row)
    rope = tables is not None
    in_specs = [
        pl.BlockSpec((bm, d), lambda i, j: (i, 0)),
        pl.BlockSpec((None, 1, d), lambda i, j: (layer, 0, 0)),
        _mod_spec(layer, 1, row_fn, d),
        _mod_spec(layer, 0, row_fn, d),
        pl.BlockSpec((d, dm), lambda i, j: (0, j)),
    ]
    args = [x, g_norm, mod, mod, w_in]
    if rope:
        nblk_seq = seq // bm
        in_specs += [pl.BlockSpec((bm, LANES), lambda i, j: (i % nblk_seq, 0))] * 2
        args += list(tables)
    out_specs = [pl.BlockSpec((bm, dm), lambda i, j: (i, 0))] * 3
    out_shape = [jax.ShapeDtypeStruct((t, dm), BF16)] * 3
    if kv32:
        out_specs.append(pl.BlockSpec((bm, dm), lambda i, j: (i, jnp.clip(j - 1, 0, 1))))
        out_shape.append(jax.ShapeDtypeStruct((t, 2 * dm), F32))
    out_specs.append(pl.BlockSpec((bm, dm), lambda i, j: (i, jnp.clip(j - 3, 0, nz - 1))))
    out_shape.append(jax.ShapeDtypeStruct((t, nz * dm), F32))
    return pl.pallas_call(
        functools.partial(_in_kernel, rope=rope, kv32=kv32),
        grid=(t // bm, ncol // dm),
        in_specs=in_specs,
        out_specs=out_specs,
        out_shape=out_shape,
        scratch_shapes=[pltpu.VMEM((bm, d), BF16)],
        compiler_params=_params("arbitrary", "arbitrary"),
        name="in_projection",
    )(*args)


def _attn_kernel(*refs, has_ctx, lam_init, kc, hp):
    q_ref, k_ref, vt_ref = refs[:3]
    pos = 3
    if has_ctx:
        ck_ref, cv_ref = refs[pos:pos + 2]
        pos += 2
    lq1, lk1, lq2, lk2, g_ref, o_ref, e_scr = refs[pos:pos + 7]

    lam = (jnp.exp(jnp.sum(lq1[...] * lk1[...], axis=-1, keepdims=True))
           - jnp.exp(jnp.sum(lq2[...] * lk2[...], axis=-1, keepdims=True)) + lam_init)

    bq = q_ref.shape[0]
    nt = (((1,), (1,)), ((), ()))

    def head(hh):
        return slice(hh * V_DIM, (hh + 1) * V_DIM)

    qs = []
    for hh in range(hp):
        q = q_ref[:, head(hh)]
        lane = lax.broadcasted_iota(jnp.int32, q.shape, 1)
        zero = jnp.zeros_like(q)
        qs.append(jnp.concatenate([jnp.where(lane < DH, q, zero), jnp.where(lane >= DH, q, zero)], axis=0))

    chunks = []
    if has_ctx:
        chunks.append((lambda hh: ck_ref[:, head(hh)], lambda hh: cv_ref[hh]))
    for c in range(k_ref.shape[0] // kc):
        rows = slice(c * kc, (c + 1) * kc)
        chunks.append((lambda hh, rows=rows: k_ref[rows, head(hh)], lambda hh, rows=rows: vt_ref[hh, :, rows]))

    def scores(ci, hh):
        return lax.dot_general(chunks[ci][0](hh), qs[hh], nt, preferred_element_type=F32)

    def exp_rows(s, m_cur, e_view):
        rows = s.shape[0]
        strip = _block(rows, EXP_STRIP)
        for r0 in range(0, rows, strip):
            e_view[r0:r0 + strip, :] = jnp.exp2(s[r0:r0 + strip] - m_cur).astype(BF16)
        return e_view[0:rows, :]

    m = [None] * hp
    acc = [None] * hp
    s_next = [scores(0, hh) for hh in range(hp)]
    for ci, (_, load_vt) in enumerate(chunks):
        for hh in range(hp):
            s = s_next[hh]
            if ci + 1 < len(chunks):
                s_next[hh] = scores(ci + 1, hh)
            smax = jnp.max(s, axis=0, keepdims=True)
            e_view = e_scr.at[ci % 2, hh]
            if m[hh] is None:
                m[hh] = smax
                e = exp_rows(s, smax, e_view)
                acc[hh] = jnp.dot(load_vt(hh), e, preferred_element_type=F32)
            else:
                m_new = jnp.maximum(m[hh], smax)
                alpha = jnp.exp2(m[hh] - m_new)
                e = exp_rows(s, m_new, e_view)
                acc[hh] = alpha * acc[hh] + jnp.dot(load_vt(hh), e, preferred_element_type=F32)
                m[hh] = m_new

    for hh in range(hp):
        inv = 1.0 / acc[hh][V_DIM:V_DIM + 1]
        val = acc[hh][:V_DIM]
        o = (val[:, :bq] * inv[:, :bq] - val[:, bq:] * (lam * inv[:, bq:])).T
        y = o * lax.rsqrt(jnp.mean(o * o, axis=-1, keepdims=True) + EPS) * g_ref[...]
        o_ref[:, head(hh)] = (y * (1.0 - lam_init)).astype(o_ref.dtype)


def _values_t(v, heads):
    lead = v.shape[:-2]
    keys = v.shape[-2]
    vt = jnp.swapaxes(v, -1, -2).reshape(*lead, heads, V_DIM, keys)
    ones = jnp.ones((*lead, heads, V_ROWS - V_DIM, keys), v.dtype)
    return jnp.concatenate([vt, ones], axis=-2)


def _attention(q, k, v, ctx_k, ctx_vt, layer, lam_p, g_subln):
    b, n, dm = q.shape
    heads = dm // V_DIM
    hp = _block(heads, HEADS_PER_STEP)
    hw = hp * V_DIM
    bq = _block(n, QUERY_BLOCK)
    kc = _block(n, KEY_CHUNK)
    has_ctx = ctx_k is not None
    lam_init = 0.8 - 0.6 * math.exp(-0.3 * layer)
    in_specs = [
        pl.BlockSpec((None, bq, hw), lambda bi, h, i: (bi, i, h)),
        pl.BlockSpec((None, n, hw), lambda bi, h, i: (bi, 0, h)),
        pl.BlockSpec((None, hp, V_ROWS, n), lambda bi, h, i: (bi, h, 0, 0)),
    ]
    args = [q, k, _values_t(v, heads)]
    if has_ctx:
        past = ctx_k.shape[2]
        in_specs += [pl.BlockSpec((None, None, past, hw), lambda bi, h, i: (bi, layer, 0, h)),
                     pl.BlockSpec((None, None, hp, V_ROWS, past), lambda bi, h, i: (bi, layer, h, 0, 0))]
        args += [ctx_k, ctx_vt]
    in_specs += [pl.BlockSpec((None, 1, DH), lambda bi, h, i: (layer, 0, 0))] * 4
    in_specs += [pl.BlockSpec((None, 1, V_DIM), lambda bi, h, i: (layer, 0, 0))]
    args += list(lam_p) + [g_subln]
    return pl.pallas_call(
        functools.partial(_attn_kernel, has_ctx=has_ctx, lam_init=lam_init, kc=kc, hp=hp),
        grid=(b, heads // hp, n // bq),
        in_specs=in_specs,
        out_specs=pl.BlockSpec((None, bq, hw), lambda bi, h, i: (bi, i, h)),
        out_shape=jax.ShapeDtypeStruct((b, n, dm), BF16),
        scratch_shapes=[pltpu.VMEM((2, hp, max(kc, past if has_ctx else 0), 2 * bq), BF16)],
        compiler_params=_params("arbitrary", "arbitrary", "arbitrary"),
        name="diff_attention",
    )(*args)


def _halo_specs(bn, halo, width, col, nseq_halo):
    per = bn // halo
    return [
        pl.BlockSpec((None, halo, width), lambda b, i: (b, jnp.maximum(i * per - 1, 0), col)),
        pl.BlockSpec((None, bn, width), lambda b, i: (b, i, col)),
        pl.BlockSpec((None, halo, width), lambda b, i: (b, jnp.minimum((i + 1) * per, nseq_halo - 1), col)),
    ]


def _edge_masks():
    i = pl.program_id(1)
    return i > 0, i < pl.num_programs(1) - 1


def _convmod_kernel(p_ref, m_ref, n_ref, w_ref, b_ref, g_ref, bl_ref, o_ref, *, halo):
    dm = o_ref.shape[-1]
    bn = o_ref.shape[0]
    taps = w_ref.shape[0]
    has_prev, has_next = _edge_masks()

    def glu(ref):
        z = ref[...]
        return z[:, :dm] * _sigmoid(z[:, dm:])

    prev = jnp.where(has_prev, glu(p_ref), 0.0)
    nxt = jnp.where(has_next, glu(n_ref), 0.0)
    win = jnp.concatenate([prev, glu(m_ref), nxt], axis=0)
    w = w_ref[...]
    left = (taps - 1) // 2
    acc = jnp.zeros((bn, dm), F32) + b_ref[...]
    shifted = {}
    for t in range(taps):
        off = halo + t - left
        sub = off % SUBLANES
        if sub not in shifted:
            shifted[sub] = pltpu.roll(win, win.shape[0] - sub, 0) if sub else win
        base = off - sub
        acc = acc + shifted[sub][base:base + bn] * w[t:t + 1]
    mu = jnp.mean(acc, axis=-1, keepdims=True)
    xc = acc - mu
    y = xc * lax.rsqrt(jnp.mean(xc * xc, axis=-1, keepdims=True) + EPS) * g_ref[...] + bl_ref[...]
    o_ref[...] = _silu(y).astype(o_ref.dtype)


def _conv_module(z, layer, w_dw, b_dw, g_ln, b_ln):
    b, n, _ = z.shape
    taps, dm = w_dw.shape[1:]
    halo = 16
    bn = _block(n, 256)
    vec = pl.BlockSpec((None, 1, dm), lambda bi, i: (layer, 0, 0))
    return pl.pallas_call(
        functools.partial(_convmod_kernel, halo=halo),
        grid=(b, n // bn),
        in_specs=_halo_specs(bn, halo, 2 * dm, 0, n // halo)
        + [pl.BlockSpec((None, taps, dm), lambda bi, i: (layer, 0, 0)), vec, vec, vec],
        out_specs=pl.BlockSpec((None, bn, dm), lambda bi, i: (bi, i, 0)),
        out_shape=jax.ShapeDtypeStruct((b, n, dm), BF16),
        compiler_params=_params("arbitrary", "arbitrary"),
        name="conformer_conv",
    )(z, z, z, w_dw, b_dw, g_ln, b_ln)


def _shortconv_kernel(p_ref, m_ref, n_ref, gb_ref, w_ref, o_ref, *, halo):
    dm = o_ref.shape[-1]
    bn = o_ref.shape[0]
    taps = w_ref.shape[0]
    has_prev, has_next = _edge_masks()

    def prod(ref):
        z = ref[...]
        return z[:, :dm] * z[:, dm:]

    prev = jnp.where(has_prev, prod(p_ref), 0.0)
    nxt = jnp.where(has_next, prod(n_ref), 0.0)
    win = jnp.concatenate([prev, prod(m_ref), nxt], axis=0)
    w = w_ref[...]
    left = (taps - 1) // 2
    acc = jnp.zeros((bn, dm), F32)
    for t in range(taps):
        acc = acc + _row_shift(win, halo + t - left, bn) * w[t:t + 1]
    o_ref[...] = (gb_ref[...] * acc).astype(o_ref.dtype)


def _short_conv(z, layer, w_dw):
    b, n, _ = z.shape
    taps, dm = w_dw.shape[1:]
    halo = SUBLANES
    bn = _block(n, 512)
    return pl.pallas_call(
        functools.partial(_shortconv_kernel, halo=halo),
        grid=(b, n // bn),
        in_specs=_halo_specs(bn, halo, 2 * dm, 1, n // halo)
        + [pl.BlockSpec((None, bn, dm), lambda bi, i: (bi, i, 4)),
           pl.BlockSpec((None, taps, dm), lambda bi, i: (layer, 0, 0))],
        out_specs=pl.BlockSpec((None, bn, dm), lambda bi, i: (bi, i, 0)),
        out_shape=jax.ShapeDtypeStruct((b, n, dm), BF16),
        compiler_params=_params("arbitrary", "arbitrary"),
        name="short_conv",
    )(z, z, z, z, w_dw)


def _scan_block(a, u, reverse):
    rows = a.shape[0]
    row = lax.broadcasted_iota(jnp.int32, a.shape, 0)
    s = 1
    while s < rows:
        if reverse:
            a_sh = pltpu.roll(a, rows - s, 0)
            u_sh = pltpu.roll(u, rows - s, 0)
            ok = row < rows - s
        else:
            a_sh = pltpu.roll(a, s, 0)
            u_sh = pltpu.roll(u, s, 0)
            ok = row >= s
        u = jnp.where(ok, a * u_sh + u, u)
        a = jnp.where(ok, a * a_sh, a)
        s *= 2
    return a, u


def _rglru_kernel(x_ref, y_ref, h0_ref, wc_ref, bc_ref, wg_ref, bg_ref, lam_ref,
                  o_ref, hl_ref, xpad, af, uf, ab, ub, *, rows):
    n, cw = x_ref.shape
    nchunk = n // rows
    pad = SUBLANES
    taps = wc_ref.shape[0]
    left = (taps - 1) // 2

    xpad[0:pad, :] = jnp.zeros((pad, cw), F32)
    xpad[pad:pad + n, :] = x_ref[...]
    xpad[pad + n:pad + n + pad, :] = jnp.zeros((pad, cw), F32)

    lam = lam_ref[...]
    softplus = jnp.maximum(-lam, 0.0) + jnp.log1p(jnp.exp(-jnp.abs(lam)))
    wc = wc_ref[...]
    bc = bc_ref[...]
    wg = wg_ref[...]
    bg = bg_ref[...]

    def gates(ci, carry):
        r0 = pl.multiple_of(ci * rows, rows)
        win = xpad[pl.ds(r0, rows + 2 * pad), :]
        xr = jnp.zeros((rows, cw), F32) + bc
        for t in range(taps):
            xr = xr + _row_shift(win, pad + t - left, rows) * wc[t:t + 1]
        g = jnp.dot(xr.astype(BF16), wg, preferred_element_type=F32) + bg
        for d, (a_ref, u_ref) in enumerate(((af, uf), (ab, ub))):
            r = _sigmoid(g[:, (2 * d) * cw:(2 * d + 1) * cw])
            i = _sigmoid(g[:, (2 * d + 1) * cw:(2 * d + 2) * cw])
            a = jnp.exp(-LRU_C * r * softplus[:, d * cw:(d + 1) * cw])
            u = jnp.sqrt(jnp.maximum(1.0 - a * a, 0.0)) * (i * xr)
            a_ref[pl.ds(r0, rows), :] = a
            u_ref[pl.ds(r0, rows), :] = u
        return carry

    lax.fori_loop(0, nchunk, gates, 0)

    def forward(ci, h):
        r0 = pl.multiple_of(ci * rows, rows)
        p, hz = _scan_block(af[pl.ds(r0, rows), :], uf[pl.ds(r0, rows), :], False)
        hs = hz + p * h
        uf[pl.ds(r0, rows), :] = hs
        return hs[rows - 1:rows]

    h_f = lax.fori_loop(0, nchunk, forward, h0_ref[0:1, :])

    def backward(cj, h):
        r0 = pl.multiple_of((nchunk - 1 - cj) * rows, rows)
        p, hz = _scan_block(ab[pl.ds(r0, rows), :], ub[pl.ds(r0, rows), :], True)
        hs = hz + p * h
        yr = y_ref[pl.ds(r0, rows), :]
        o_ref[pl.ds(r0, rows), :] = ((uf[pl.ds(r0, rows), :] + hs) * _gelu_tanh(yr)).astype(o_ref.dtype)
        return hs[0:1]

    h_b = lax.fori_loop(0, nchunk, backward, h0_ref[1:2, :])
    hl_ref[0:1, :] = h_f
    hl_ref[1:2, :] = h_b


def _rglru(z, h0, layer, w_conv4, b_conv4, wg_bd, bg_bd, lam_bd):
    b, n, zc = z.shape
    dm = w_conv4.shape[-1]
    cw = LANES
    nck = dm // cw
    taps = w_conv4.shape[1]
    rows = _block(n, 256)
    xcol = 5 * nck
    ycol = 6 * nck
    seq = pltpu.VMEM((n, cw), F32)
    return pl.pallas_call(
        functools.partial(_rglru_kernel, rows=rows),
        grid=(b, nck),
        in_specs=[
            pl.BlockSpec((None, n, cw), lambda bi, c: (bi, 0, xcol + c)),
            pl.BlockSpec((None, n, cw), lambda bi, c: (bi, 0, ycol + c)),
            pl.BlockSpec((None, 2, cw), lambda bi, c: (bi, 0, c)),
            pl.BlockSpec((None, taps, cw), lambda bi, c: (layer, 0, c)),
            pl.BlockSpec((None, 1, cw), lambda bi, c: (layer, 0, c)),
            pl.BlockSpec((None, None, cw, 4 * cw), lambda bi, c: (layer, c, 0, 0)),
            pl.BlockSpec((None, None, 1, 4 * cw), lambda bi, c: (layer, c, 0, 0)),
            pl.BlockSpec((None, None, 1, 2 * cw), lambda bi, c: (layer, c, 0, 0)),
        ],
        out_specs=[
            pl.BlockSpec((None, n, cw), lambda bi, c: (bi, 0, c)),
            pl.BlockSpec((None, 2, cw), lambda bi, c: (bi, 0, c)),
        ],
        out_shape=[
            jax.ShapeDtypeStruct((b, n, dm), BF16),
            jax.ShapeDtypeStruct((b, 2, dm), F32),
        ],
        scratch_shapes=[pltpu.VMEM((n + 2 * SUBLANES, cw), F32), seq, seq, seq, seq],
        compiler_params=_params("arbitrary", "arbitrary"),
        name="rglru",
    )(z, z, h0, w_conv4, b_conv4, wg_bd, bg_bd, lam_bd)


def _merge_kernel(x_ref, g_ref, sc_ref, sh_ref, ya, yb, yc, yd, g0, g1, g2, g3, b0, b1, b2, b3,
                  o_ref, h_scr):
    bm = x_ref.shape[0]
    j = pl.program_id(1)

    def merge_rows(h, r0, rows):
        acc = None
        for y_ref, wg_ref, wb_ref in ((ya, g0, b0), (yb, g1, b1), (yc, g2, b2), (yd, g3, b3)):
            gate = _sigmoid(jnp.dot(h, wg_ref[...], preferred_element_type=F32))
            term = gate * jnp.dot(y_ref[r0:r0 + rows, :], wb_ref[...], preferred_element_type=F32)
            acc = term if acc is None else acc + term
        o_ref[r0:r0 + rows, :] = acc.astype(o_ref.dtype)

    @pl.when(j == 0)
    def _():
        rc = _block(bm, NORM_CHUNK)
        for r0 in range(0, bm, rc):
            h = _adaln(x_ref[r0:r0 + rc, :], g_ref[...], sc_ref[...], sh_ref[...])
            h_scr[r0:r0 + rc, :] = h
            merge_rows(h, r0, rc)

    @pl.when(j > 0)
    def _():
        merge_rows(h_scr[...], 0, bm)


def _merge(x, mod, layer, rows_per_mod, base_row, g_norm, ys, w_gate, w_branch):
    t, d = x.shape
    dm = d // 4
    bm, row_fn = _row_blocks(rows_per_mod, base_row)
    bn = _block(d, 256)
    ncb = d // bn
    in_specs = [
        pl.BlockSpec((bm, d), lambda i, j: (i, 0)),
        pl.BlockSpec((None, 1, d), lambda i, j: (layer, 0, 0)),
        _mod_spec(layer, 1, row_fn, d),
        _mod_spec(layer, 0, row_fn, d),
    ]
    in_specs += [pl.BlockSpec((bm, dm), lambda i, j: (i, 0))] * 4
    in_specs += [pl.BlockSpec((d, bn), functools.partial(lambda i, j, jb: (0, jb * ncb + j), jb=jb))
                 for jb in range(4)]
    in_specs += [pl.BlockSpec((None, None, dm, bn), functools.partial(lambda i, j, jb: (layer, jb, 0, j), jb=jb))
                 for jb in range(4)]
    return pl.pallas_call(
        _merge_kernel,
        grid=(t // bm, ncb),
        in_specs=in_specs,
        out_specs=pl.BlockSpec((bm, bn), lambda i, j: (i, j)),
        out_shape=jax.ShapeDtypeStruct((t, d), BF16),
        scratch_shapes=[pltpu.VMEM((bm, d), BF16)],
        compiler_params=_params("arbitrary", "arbitrary"),
        name="gated_merge",
    )(x, g_norm, mod, mod, *ys, *([w_gate] * 4), *([w_branch] * 4))


def _resid_kernel(a_ref, w_ref, x_ref, gt_ref, o_ref):
    o_ref[...] = x_ref[...] + gt_ref[...] * jnp.dot(a_ref[...], w_ref[...], preferred_element_type=F32)


def _residual_proj(a, w, x, mod, layer, which, rows_per_mod, base_row, name):
    t, kdim = a.shape
    d = x.shape[1]
    bm, row_fn = _row_blocks(rows_per_mod, base_row)
    bn = _block(d, 512)
    return pl.pallas_call(
        _resid_kernel,
        grid=(t // bm, d // bn),
        in_specs=[
            pl.BlockSpec((bm, kdim), lambda i, j: (i, 0)),
            pl.BlockSpec((kdim, bn), lambda i, j: (0, j)),
            pl.BlockSpec((bm, bn), lambda i, j: (i, j)),
            _mod_spec(layer, which, row_fn, bn, col=True),
        ],
        out_specs=pl.BlockSpec((bm, bn), lambda i, j: (i, j)),
        out_shape=jax.ShapeDtypeStruct((t, d), F32),
        compiler_params=_params("arbitrary", "arbitrary"),
        name=name,
    )(a, w, x, mod)


def _ffn_up_kernel(p_ref, m_ref, n_ref, g_ref, sc_ref, sh_ref, wa_ref, wu_ref, wc_ref, bc_ref,
                   o_ref, h_scr, *, seq):
    bm = m_ref.shape[0]
    halo = p_ref.shape[0]
    i = pl.program_id(0)
    j = pl.program_id(1)

    def norm_rows(ref, r0, rows, lo):
        h_scr[lo:lo + rows, :] = _adaln(ref[r0:r0 + rows, :], g_ref[...], sc_ref[...], sh_ref[...])

    def ffn_rows(r0, rows):
        h = h_scr[r0:r0 + rows + 2 * halo, :]
        a = jnp.dot(h, wa_ref[...], preferred_element_type=F32)
        u = jnp.dot(h[halo:halo + rows], wu_ref[...], preferred_element_type=F32)
        tok = i * bm + r0 + lax.broadcasted_iota(jnp.int32, u.shape, 0)
        pos = (tok & (seq - 1)) if seq & (seq - 1) == 0 else lax.rem(tok, seq)
        w = wc_ref[...]
        taps = w.shape[0]
        left = (taps - 1) // 2
        acc = jnp.zeros(u.shape, F32) + bc_ref[...]
        for t in range(taps):
            delta = t - left
            tap = _row_shift(a, halo + delta, rows) * w[t:t + 1]
            if delta < 0:
                tap = jnp.where(pos >= -delta, tap, 0.0)
            elif delta > 0:
                tap = jnp.where(pos < seq - delta, tap, 0.0)
            acc = acc + tap
        o_ref[r0:r0 + rows, :] = (_silu(acc) * u).astype(o_ref.dtype)

    @pl.when(j == 0)
    def _():
        rc = _block(bm, NORM_CHUNK)
        norm_rows(p_ref, 0, halo, 0)
        norm_rows(m_ref, 0, rc, halo)
        for r0 in range(0, bm, rc):
            if r0 + rc < bm:
                norm_rows(m_ref, r0 + rc, rc, halo + r0 + rc)
            else:
                norm_rows(n_ref, 0, halo, halo + bm)
            ffn_rows(r0, rc)

    @pl.when(j > 0)
    def _():
        ffn_rows(0, bm)


def _ffn_up(x, mod, layer, rows_per_mod, base_row, g_norm, wa, wu, w_conv, b_conv, seq):
    t, d = x.shape
    ffp = wa.shape[-1]
    taps = w_conv.shape[1]
    halo = 16
    bm, row_fn = _row_blocks(rows_per_mod, base_row)
    bn = _block(ffp, 512)
    per = bm // halo
    nhalo = t // halo
    return pl.pallas_call(
        functools.partial(_ffn_up_kernel, seq=seq),
        grid=(t // bm, ffp // bn),
        in_specs=[
            pl.BlockSpec((halo, d), lambda i, j: (jnp.maximum(i * per - 1, 0), 0)),
            pl.BlockSpec((bm, d), lambda i, j: (i, 0)),
            pl.BlockSpec((halo, d), lambda i, j: (jnp.minimum((i + 1) * per, nhalo - 1), 0)),
            pl.BlockSpec((None, 1, d), lambda i, j: (layer, 0, 0)),
            _mod_spec(layer, 4, row_fn, d),
            _mod_spec(layer, 3, row_fn, d),
            pl.BlockSpec((None, d, bn), lambda i, j: (layer, 0, j)),
            pl.BlockSpec((None, d, bn), lambda i, j: (layer, 0, j)),
            pl.BlockSpec((None, taps, bn), lambda i, j: (layer, 0, j)),
            pl.BlockSpec((None, 1, bn), lambda i, j: (layer, 0, j)),
        ],
        out_specs=pl.BlockSpec((bm, bn), lambda i, j: (i, j)),
        out_shape=jax.ShapeDtypeStruct((t, ffp), BF16),
        scratch_shapes=[pltpu.VMEM((bm + 2 * halo, d), BF16)],
        compiler_params=_params("arbitrary", "arbitrary"),
        name="ffn_up",
    )(x, x, x, g_norm, mod, mod, wa, wu, w_conv, b_conv)


def _final_norm_kernel(x_ref, g_ref, o_ref):
    x = x_ref[...]
    o_ref[...] = x * lax.rsqrt(jnp.mean(x * x, axis=-1, keepdims=True) + EPS) * g_ref[...]


def _final_norm(x, g):
    t, d = x.shape
    bm = _block(t, 512)
    return pl.pallas_call(
        _final_norm_kernel,
        grid=(t // bm,),
        in_specs=[pl.BlockSpec((bm, d), lambda i: (i, 0)), pl.BlockSpec((1, d), lambda i: (0, 0))],
        out_specs=pl.BlockSpec((bm, d), lambda i: (i, 0)),
        out_shape=jax.ShapeDtypeStruct((t, d), F32),
        compiler_params=_params("arbitrary"),
        name="final_norm",
    )(x, g.reshape(1, d))


def _rope_tables(n):
    pairs = DH // 4
    rows = n // GRID_W
    t_row = jnp.repeat(jnp.arange(rows), GRID_W).astype(F32)
    t_col = jnp.tile(jnp.arange(GRID_W), rows).astype(F32)
    inv = jnp.power(ROPE_BASE, -jnp.arange(pairs, dtype=F32) / pairs)
    ang_row = t_row[:, None] * inv
    ang_col = t_col[:, None] * inv
    ang = jnp.concatenate([ang_row, ang_row, ang_col, ang_col], axis=1)
    sign = np.tile(np.repeat(np.array([-1.0, 1.0], np.float32), pairs), 2)
    cos = jnp.cos(ang)
    sin = jnp.sin(ang) * sign
    reps = LANES // DH
    return jnp.tile(cos, (1, reps)), jnp.tile(sin, (1, reps))


def _block_diag_gates(w_rg_a, b_rg_a, w_rg_x, b_rg_x, lru_lambda, dm):
    depth = w_rg_a.shape[0]
    cw = LANES
    nck = dm // cw
    per = cw // BS_R

    def dense(w):
        w = w.reshape(depth, 2, nck, per, BS_R, BS_R)
        eye = jnp.eye(per, dtype=w.dtype)
        full = jnp.einsum('ldcpij,pq->ldcpiqj', w, eye)
        return full.reshape(depth, 2, nck, cw, cw)

    wa, wx = dense(w_rg_a), dense(w_rg_x)
    wg = jnp.concatenate([wa[:, 0], wx[:, 0], wa[:, 1], wx[:, 1]], axis=-1).astype(BF16)

    def vec(v):
        return v.reshape(depth, 2, nck, 1, cw)

    ba, bx = vec(b_rg_a), vec(b_rg_x)
    bg = jnp.concatenate([ba[:, 0], bx[:, 0], ba[:, 1], bx[:, 1]], axis=-1)
    lam = vec(lru_lambda)
    lam = jnp.concatenate([lam[:, 0], lam[:, 1]], axis=-1)
    return wg, bg, lam


def kernel(x_prompt, x_sample, cache_k, cache_v, state_lru, c, c_ctx, w_mod, b_mod, g_norm1, g_norm2, g_final, w_in, lam_q1, lam_k1, lam_q2, lam_k2, g_subln, w_dw31, b_dw31, g_ln_conv, b_ln_conv, w_dw3, w_conv4, b_conv4, w_rg_a, b_rg_a, w_rg_x, b_rg_x, lru_lambda, w_branch, w_out, w_ffn_up, w_ffn_conv, b_ffn_conv, w_ffn_down):
    depth, d, _ = w_in.shape
    dm = d // 4
    bc, sc, _ = x_prompt.shape
    bd, sd, _ = x_sample.shape
    past = cache_k.shape[2]
    d_ff = w_ffn_conv.shape[-1]
    ffp = -(-d_ff // 512) * 512

    rows = -(-(bd + 1) // SUBLANES) * SUBLANES
    cvec = jnp.zeros((rows, d), F32).at[:bd].set(c).at[bd].set(c_ctx)
    mod = _modulation(cvec, w_mod, b_mod).reshape(depth, rows, 6, 1, d)

    qkvz = jnp.concatenate([w_in[:, :, :5 * dm], w_in[:, :, 6 * dm:8 * dm], w_in[:, :, 5 * dm:6 * dm],
                            w_in[:, :, 8 * dm:10 * dm]], axis=-1).astype(BF16)
    w_gate = w_in[:, :, 10 * dm:].astype(BF16)
    w_branch_b = w_branch.astype(BF16)
    w_out_b = w_out.astype(BF16)
    padc = [(0, 0), (0, 0), (0, ffp - d_ff)]
    w_up_a = jnp.pad(w_ffn_up[:, :, :d_ff], padc).astype(BF16)
    w_up_u = jnp.pad(w_ffn_up[:, :, d_ff:], padc).astype(BF16)
    w_fc = jnp.pad(w_ffn_conv, padc)
    b_fc = jnp.pad(b_ffn_conv, [(0, 0), (0, ffp - d_ff)]).reshape(depth, 1, ffp)
    w_down = jnp.pad(w_ffn_down, [(0, 0), (0, ffp - d_ff), (0, 0)]).astype(BF16)
    wg_bd, bg_bd, lam_bd = _block_diag_gates(w_rg_a, b_rg_a, w_rg_x, b_rg_x, lru_lambda, dm)

    def v3(p):
        return p.reshape(depth, 1, p.shape[-1])

    g1, g2 = v3(g_norm1), v3(g_norm2)
    lam_p = [v3(lam_q1), v3(lam_k1), v3(lam_q2), v3(lam_k2)]
    g_sub = v3(g_subln)
    b31, gln, bln, b4 = v3(b_dw31), v3(g_ln_conv), v3(b_ln_conv), v3(b_conv4)

    def layer_pass(x, layer, b, n, rpm, base, tables, ctx_k, ctx_v, h0, want_kv):
        outs = _in_projection(x, mod, layer, rpm, base, g1, qkvz[layer], tables, n, want_kv)
        if want_kv:
            q, k, v, kv32, z = outs
        else:
            q, k, v, z = outs
            kv32 = None
        q, k, v, z = (a.reshape(b, n, -1) for a in (q, k, v, z))
        y_a = _attention(q, k, v, ctx_k, ctx_v, layer, lam_p, g_sub)
        y_b = _conv_module(z, layer, w_dw31, b31, gln, bln)
        y_c = _short_conv(z, layer, w_dw3)
        y_d, h_last = _rglru(z, h0, layer, w_conv4, b4, wg_bd, bg_bd, lam_bd)
        ys = [y.reshape(b * n, dm) for y in (y_a, y_b, y_c, y_d)]
        merged = _merge(x, mod, layer, rpm, base, g1, ys, w_gate[layer], w_branch_b)
        x1 = _residual_proj(merged, w_out_b[layer], x, mod, layer, 2, rpm, base, "out_projection")
        act = _ffn_up(x1, mod, layer, rpm, base, g2, w_up_a, w_up_u, w_fc, b_fc, n)
        x2 = _residual_proj(act, w_down[layer], x1, mod, layer, 5, rpm, base, "ffn_down")
        return x2, kv32, h_last

    xp = x_prompt.reshape(bc * sc, d)
    zero_state = jnp.zeros((bc, 2, dm), F32)
    ks, vs, ss = [], [], []
    for layer in range(depth):
        xp, kv32, h_last = layer_pass(xp, layer, bc, sc, bc * sc, bd, None, None, None, zero_state, True)
        kv32 = kv32.reshape(bc, sc, 2 * dm)
        ks.append(kv32[:, :, :dm])
        vs.append(kv32[:, :, dm:])
        ss.append(h_last)
    heads = dm // V_DIM
    y_prompt = _final_norm(xp, g_final).reshape(bc, sc, d)
    new_cache_k = jnp.stack(ks, axis=1).reshape(bc, depth, sc, heads, 2, DH)
    new_cache_v = jnp.stack(vs, axis=1).reshape(bc, depth, sc, heads, V_DIM)
    new_state = jnp.stack(ss, axis=1)

    tables = _rope_tables(sd)
    ck = cache_k.reshape(bd, depth, past, dm).astype(BF16)
    cv = _values_t(cache_v.reshape(bd, depth, past, dm).astype(BF16), dm // V_DIM)
    xs = x_sample.reshape(bd * sd, d)
    for layer in range(depth):
        xs, _, _ = layer_pass(xs, layer, bd, sd, sd, 0, tables, ck, cv, state_lru[:, layer], False)
    y_sample = _final_norm(xs, g_final).reshape(bd, sd, d)
    return (y_prompt, y_sample, new_cache_k, new_cache_v, new_state)
```

```python
import functools
import math

import jax
import jax.numpy as jnp
import numpy as np
from jax import lax
from jax.experimental import pallas as pl
from jax.experimental.pallas import tpu as pltpu

F32 = jnp.float32
BF16 = jnp.bfloat16

EPS = 1e-6
ROPE_BASE = 10000.0
GRID_W = 64
LRU_C = 8.0
V_DIM = 128
DH = V_DIM // 2
V_ROWS = V_DIM + 16
BS_R = 64
LANES = 128
SUBLANES = 8
VMEM_LIMIT_BYTES = 56 * 1024 * 1024
ROW_BLOCK = 1024
KEY_CHUNK = 1024
QUERY_BLOCK = 256
HEADS_PER_STEP = 2
EXP_STRIP = 32
NORM_CHUNK = 256


def _params(*sem):
    return pltpu.CompilerParams(dimension_semantics=sem, vmem_limit_bytes=VMEM_LIMIT_BYTES)


def _block(n, pref):
    b = min(n, pref)
    while n % b:
        b //= 2
    return b


def _row_blocks(rows_per_mod, base_row, pref=ROW_BLOCK):
    bm = _block(rows_per_mod, pref)
    return bm, (lambda i: base_row + (i * bm) // rows_per_mod)


def _sigmoid(x):
    return 1.0 / (1.0 + jnp.exp(-x))


def _silu(x):
    return x * _sigmoid(x)


def _gelu_tanh(x):
    return 0.5 * x * (1.0 + jnp.tanh(math.sqrt(2.0 / math.pi) * (x + 0.044715 * (x * x * x))))


def _adaln(x, g, sc, sh):
    y = x * lax.rsqrt(jnp.mean(x * x, axis=-1, keepdims=True) + EPS) * g
    return (y * (1.0 + sc) + sh).astype(BF16)


def _row_shift(win, off, rows):
    n = win.shape[0]
    sub = off % SUBLANES
    base = off - sub
    if sub:
        win = pltpu.roll(win, n - sub, 0)
    return win[base:base + rows]


def _mod_kernel(c_ref, w_ref, b_ref, o_ref):
    c = c_ref[...]
    s = _silu(c).astype(BF16)
    o_ref[...] = jnp.dot(s, w_ref[...].astype(BF16), preferred_element_type=F32) + b_ref[...]


def _modulation(cvec, w_mod, b_mod):
    depth, d, n6 = w_mod.shape
    rows = cvec.shape[0]
    bn = _block(n6, 1024)
    return pl.pallas_call(
        _mod_kernel,
        grid=(depth, n6 // bn),
        in_specs=[
            pl.BlockSpec((rows, d), lambda l, n: (0, 0)),
            pl.BlockSpec((None, d, bn), lambda l, n: (l, 0, n)),
            pl.BlockSpec((None, 1, bn), lambda l, n: (l, 0, n)),
        ],
        out_specs=pl.BlockSpec((None, rows, bn), lambda l, n: (l, 0, n)),
        out_shape=jax.ShapeDtypeStruct((depth, rows, n6), F32),
        compiler_params=_params("arbitrary", "arbitrary"),
        name="modulation",
    )(cvec, w_mod, b_mod.reshape(depth, 1, n6))


def _mod_spec(layer, which, row_fn, bn, col=False):
    def full(i, j):
        return (layer, row_fn(i), which, 0, 0)

    def cols(i, j):
        return (layer, row_fn(i), which, 0, j)

    return pl.BlockSpec((None, None, None, 1, bn), cols if col else full)


def _rope_store(out_ref, r0, acc, cos, sin, scale):
    rows = acc.shape[0]
    lane = lax.broadcasted_iota(jnp.int32, (rows, LANES), 1)
    first_half = (lane % 32) < 16
    for c in range(acc.shape[1] // LANES):
        a = acc[:, c * LANES:(c + 1) * LANES]
        up = pltpu.roll(a, LANES - 16, 1)
        down = pltpu.roll(a, 16, 1)
        r = a * cos + jnp.where(first_half, up, down) * sin
        if scale != 1.0:
            r = r * scale
        out_ref[r0:r0 + rows, c * LANES:(c + 1) * LANES] = r.astype(out_ref.dtype)


def _in_kernel(*refs, rope, kv32):
    x_ref, g_ref, sc_ref, sh_ref, w_ref = refs[:5]
    pos = 5
    if rope:
        cos_ref, sin_ref = refs[pos:pos + 2]
        pos += 2
    q_out, k_out, v_out = refs[pos:pos + 3]
    pos += 3
    if kv32:
        kv_out = refs[pos]
        pos += 1
    z_out, h_scr = refs[pos:pos + 2]

    j = pl.program_id(1)
    q_scale = DH ** -0.5 * math.log2(math.e)

    @pl.when(j == 0)
    def _():
        bm = x_ref.shape[0]
        rc = _block(bm, NORM_CHUNK)
        for r0 in range(0, bm, rc):
            h = _adaln(x_ref[r0:r0 + rc, :], g_ref[...], sc_ref[...], sh_ref[...])
            h_scr[r0:r0 + rc, :] = h
            acc = jnp.dot(h, w_ref[...], preferred_element_type=F32)
            if rope:
                _rope_store(q_out, r0, acc, cos_ref[r0:r0 + rc, :], sin_ref[r0:r0 + rc, :], q_scale)
            else:
                q_out[r0:r0 + rc, :] = (acc * q_scale).astype(BF16)

    @pl.when(j == 1)
    def _():
        bm = x_ref.shape[0]
        rc = _block(bm, NORM_CHUNK) if rope else bm
        for r0 in range(0, bm, rc):
            acc = jnp.dot(h_scr[r0:r0 + rc, :], w_ref[...], preferred_element_type=F32)
            if rope:
                _rope_store(k_out, r0, acc, cos_ref[r0:r0 + rc, :], sin_ref[r0:r0 + rc, :], 1.0)
            else:
                k_out[r0:r0 + rc, :] = acc.astype(BF16)
            if kv32:
                kv_out[r0:r0 + rc, :] = acc

    @pl.when(j > 1)
    def _():
        acc = jnp.dot(h_scr[...], w_ref[...], preferred_element_type=F32)

        @pl.when(j == 2)
        def _():
            v_out[...] = acc.astype(BF16)
            if kv32:
                kv_out[...] = acc

        @pl.when(j >= 3)
        def _():
            z_out[...] = acc


def _in_projection(x, mod, layer, rows_per_mod, base_row, g_norm, w_in, tables, seq, kv32):
    t, d = x.shape
    dm = d // 4
    ncol = w_in.shape[2]
    nz = ncol // dm - 3
    bm, row_fn = _row_blocks(rows_per_mod, base_row)
    rope = tables is not None
    in_specs = [
        pl.BlockSpec((bm, d), lambda i, j: (i, 0)),
        pl.BlockSpec((None, 1, d), lambda i, j: (layer, 0, 0)),
        _mod_spec(layer, 1, row_fn, d),
        _mod_spec(layer, 0, row_fn, d),
        pl.BlockSpec((None, d, dm), lambda i, j: (layer, 0, j)),
    ]
    args = [x, g_norm, mod, mod, w_in]
    if rope:
        nblk_seq = seq // bm
        in_specs += [pl.BlockSpec((bm, LANES), lambda i, j: (i % nblk_seq, 0))] * 2
        args += list(tables)
    out_specs = [pl.BlockSpec((bm, dm), lambda i, j: (i, 0))] * 3
    out_shape = [jax.ShapeDtypeStruct((t, dm), BF16)] * 3
    if kv32:
        out_specs.append(pl.BlockSpec((bm, dm), lambda i, j: (i, jnp.clip(j - 1, 0, 1))))
        out_shape.append(jax.ShapeDtypeStruct((t, 2 * dm), F32))
    out_specs.append(pl.BlockSpec((bm, dm), lambda i, j: (i, jnp.clip(j - 3, 0, nz - 1))))
    out_shape.append(jax.ShapeDtypeStruct((t, nz * dm), F32))
    return pl.pallas_call(
        functools.partial(_in_kernel, rope=rope, kv32=kv32),
        grid=(t // bm, ncol // dm),
        in_specs=in_specs,
        out_specs=out_specs,
        out_shape=out_shape,
        scratch_shapes=[pltpu.VMEM((bm, d), BF16)],
        compiler_params=_params("arbitrary", "arbitrary"),
        name="in_projection",
    )(*args)


def _attn_kernel(*refs, has_ctx, lam_init, kc, hp):
    q_ref, k_ref, vt_ref = refs[:3]
    pos = 3
    if has_ctx:
        ck_ref, cv_ref = refs[pos:pos + 2]
        pos += 2
    lq1, lk1, lq2, lk2, g_ref, o_ref, e_scr = refs[pos:pos + 7]

    lam = (jnp.exp(jnp.sum(lq1[...] * lk1[...], axis=-1, keepdims=True))
           - jnp.exp(jnp.sum(lq2[...] * lk2[...], axis=-1, keepdims=True)) + lam_init)

    bq = q_ref.shape[0]
    nt = (((1,), (1,)), ((), ()))

    def head(hh):
        return slice(hh * V_DIM, (hh + 1) * V_DIM)

    qs = []
    for hh in range(hp):
        q = q_ref[:, head(hh)]
        lane = lax.broadcasted_iota(jnp.int32, q.shape, 1)
        zero = jnp.zeros_like(q)
        qs.append(jnp.concatenate([jnp.where(lane < DH, q, zero), jnp.where(lane >= DH, q, zero)], axis=0))

    chunks = []
    if has_ctx:
        chunks.append((lambda hh: ck_ref[:, head(hh)], lambda hh: cv_ref[hh]))
    for c in range(k_ref.shape[0] // kc):
        rows = slice(c * kc, (c + 1) * kc)
        chunks.append((lambda hh, rows=rows: k_ref[rows, head(hh)], lambda hh, rows=rows: vt_ref[hh, :, rows]))

    def scores(ci, hh):
        return lax.dot_general(chunks[ci][0](hh), qs[hh], nt, preferred_element_type=F32)

    def exp_rows(s, m_cur, e_view):
        rows = s.shape[0]
        strip = _block(rows, EXP_STRIP)
        for r0 in range(0, rows, strip):
            e_view[r0:r0 + strip, :] = jnp.exp2(s[r0:r0 + strip] - m_cur).astype(BF16)
        return e_view[0:rows, :]

    m = [None] * hp
    acc = [None] * hp
    s_next = [scores(0, hh) for hh in range(hp)]
    for ci, (_, load_vt) in enumerate(chunks):
        for hh in range(hp):
            s = s_next[hh]
            if ci + 1 < len(chunks):
                s_next[hh] = scores(ci + 1, hh)
            smax = jnp.max(s, axis=0, keepdims=True)
            e_view = e_scr.at[ci % 2, hh]
            if m[hh] is None:
                m[hh] = smax
                e = exp_rows(s, smax, e_view)
                acc[hh] = jnp.dot(load_vt(hh), e, preferred_element_type=F32)
            else:
                m_new = jnp.maximum(m[hh], smax)
                alpha = jnp.exp2(m[hh] - m_new)
                e = exp_rows(s, m_new, e_view)
                acc[hh] = alpha * acc[hh] + jnp.dot(load_vt(hh), e, preferred_element_type=F32)
                m[hh] = m_new

    for hh in range(hp):
        inv = 1.0 / acc[hh][V_DIM:V_DIM + 1]
        val = acc[hh][:V_DIM]
        o = (val[:, :bq] * inv[:, :bq] - val[:, bq:] * (lam * inv[:, bq:])).T
        y = o * lax.rsqrt(jnp.mean(o * o, axis=-1, keepdims=True) + EPS) * g_ref[...]
        o_ref[:, head(hh)] = (y * (1.0 - lam_init)).astype(o_ref.dtype)


def _values_t(v, heads):
    lead = v.shape[:-2]
    keys = v.shape[-2]
    vt = jnp.swapaxes(v, -1, -2).reshape(*lead, heads, V_DIM, keys)
    ones = jnp.ones((*lead, heads, V_ROWS - V_DIM, keys), v.dtype)
    return jnp.concatenate([vt, ones], axis=-2)


def _attention(q, k, v, ctx_k, ctx_vt, layer, lam_p, g_subln):
    b, n, dm = q.shape
    heads = dm // V_DIM
    hp = _block(heads, HEADS_PER_STEP)
    hw = hp * V_DIM
    bq = _block(n, QUERY_BLOCK)
    kc = _block(n, KEY_CHUNK)
    has_ctx = ctx_k is not None
    lam_init = 0.8 - 0.6 * math.exp(-0.3 * layer)
    in_specs = [
        pl.BlockSpec((None, bq, hw), lambda bi, h, i: (bi, i, h)),
        pl.BlockSpec((None, n, hw), lambda bi, h, i: (bi, 0, h)),
        pl.BlockSpec((None, hp, V_ROWS, n), lambda bi, h, i: (bi, h, 0, 0)),
    ]
    args = [q, k, _values_t(v, heads)]
    if has_ctx:
        past = ctx_k.shape[2]
        in_specs += [pl.BlockSpec((None, None, past, hw), lambda bi, h, i: (bi, layer, 0, h)),
                     pl.BlockSpec((None, None, hp, V_ROWS, past), lambda bi, h, i: (bi, layer, h, 0, 0))]
        args += [ctx_k, ctx_vt]
    in_specs += [pl.BlockSpec((None, 1, DH), lambda bi, h, i: (layer, 0, 0))] * 4
    in_specs += [pl.BlockSpec((None, 1, V_DIM), lambda bi, h, i: (layer, 0, 0))]
    args += list(lam_p) + [g_subln]
    return pl.pallas_call(
        functools.partial(_attn_kernel, has_ctx=has_ctx, lam_init=lam_init, kc=kc, hp=hp),
        grid=(b, heads // hp, n // bq),
        in_specs=in_specs,
        out_specs=pl.BlockSpec((None, bq, hw), lambda bi, h, i: (bi, i, h)),
        out_shape=jax.ShapeDtypeStruct((b, n, dm), BF16),
        scratch_shapes=[pltpu.VMEM((2, hp, max(kc, past if has_ctx else 0), 2 * bq), BF16)],
        compiler_params=_params("arbitrary", "arbitrary", "arbitrary"),
        name="diff_attention",
    )(*args)


def _halo_specs(bn, halo, width, col, nseq_halo):
    per = bn // halo
    return [
        pl.BlockSpec((None, halo, width), lambda b, i: (b, jnp.maximum(i * per - 1, 0), col)),
        pl.BlockSpec((None, bn, width), lambda b, i: (b, i, col)),
        pl.BlockSpec((None, halo, width), lambda b, i: (b, jnp.minimum((i + 1) * per, nseq_halo - 1), col)),
    ]


def _edge_masks():
    i = pl.program_id(1)
    return i > 0, i < pl.num_programs(1) - 1


def _convmod_kernel(p_ref, m_ref, n_ref, w_ref, b_ref, g_ref, bl_ref, o_ref, *, halo):
    dm = o_ref.shape[-1]
    bn = o_ref.shape[0]
    taps = w_ref.shape[0]
    has_prev, has_next = _edge_masks()

    def glu(ref):
        z = ref[...]
        return z[:, :dm] * _sigmoid(z[:, dm:])

    prev = jnp.where(has_prev, glu(p_ref), 0.0)
    nxt = jnp.where(has_next, glu(n_ref), 0.0)
    win = jnp.concatenate([prev, glu(m_ref), nxt], axis=0)
    w = w_ref[...]
    left = (taps - 1) // 2
    acc = jnp.zeros((bn, dm), F32) + b_ref[...]
    shifted = {}
    for t in range(taps):
        off = halo + t - left
        sub = off % SUBLANES
        if sub not in shifted:
            shifted[sub] = pltpu.roll(win, win.shape[0] - sub, 0) if sub else win
        base = off - sub
        acc = acc + shifted[sub][base:base + bn] * w[t:t + 1]
    mu = jnp.mean(acc, axis=-1, keepdims=True)
    xc = acc - mu
    y = xc * lax.rsqrt(jnp.mean(xc * xc, axis=-1, keepdims=True) + EPS) * g_ref[...] + bl_ref[...]
    o_ref[...] = _silu(y).astype(o_ref.dtype)


def _conv_module(z, layer, w_dw, b_dw, g_ln, b_ln):
    b, n, _ = z.shape
    taps, dm = w_dw.shape[1:]
    halo = 16
    bn = _block(n, 256)
    vec = pl.BlockSpec((None, 1, dm), lambda bi, i: (layer, 0, 0))
    return pl.pallas_call(
        functools.partial(_convmod_kernel, halo=halo),
        grid=(b, n // bn),
        in_specs=_halo_specs(bn, halo, 2 * dm, 0, n // halo)
        + [pl.BlockSpec((None, taps, dm), lambda bi, i: (layer, 0, 0)), vec, vec, vec],
        out_specs=pl.BlockSpec((None, bn, dm), lambda bi, i: (bi, i, 0)),
        out_shape=jax.ShapeDtypeStruct((b, n, dm), BF16),
        compiler_params=_params("arbitrary", "arbitrary"),
        name="conformer_conv",
    )(z, z, z, w_dw, b_dw, g_ln, b_ln)


def _shortconv_kernel(p_ref, m_ref, n_ref, gb_ref, w_ref, o_ref, *, halo):
    dm = o_ref.shape[-1]
    bn = o_ref.shape[0]
    taps = w_ref.shape[0]
    has_prev, has_next = _edge_masks()

    def prod(ref):
        z = ref[...]
        return z[:, :dm] * z[:, dm:]

    prev = jnp.where(has_prev, prod(p_ref), 0.0)
    nxt = jnp.where(has_next, prod(n_ref), 0.0)
    win = jnp.concatenate([prev, prod(m_ref), nxt], axis=0)
    w = w_ref[...]
    left = (taps - 1) // 2
    acc = jnp.zeros((bn, dm), F32)
    for t in range(taps):
        acc = acc + _row_shift(win, halo + t - left, bn) * w[t:t + 1]
    o_ref[...] = (gb_ref[...] * acc).astype(o_ref.dtype)


def _short_conv(z, layer, w_dw):
    b, n, _ = z.shape
    taps, dm = w_dw.shape[1:]
    halo = SUBLANES
    bn = _block(n, 512)
    return pl.pallas_call(
        functools.partial(_shortconv_kernel, halo=halo),
        grid=(b, n // bn),
        in_specs=_halo_specs(bn, halo, 2 * dm, 1, n // halo)
        + [pl.BlockSpec((None, bn, dm), lambda bi, i: (bi, i, 4)),
           pl.BlockSpec((None, taps, dm), lambda bi, i: (layer, 0, 0))],
        out_specs=pl.BlockSpec((None, bn, dm), lambda bi, i: (bi, i, 0)),
        out_shape=jax.ShapeDtypeStruct((b, n, dm), BF16),
        compiler_params=_params("arbitrary", "arbitrary"),
        name="short_conv",
    )(z, z, z, z, w_dw)


def _scan_block(a, u, reverse):
    rows = a.shape[0]
    row = lax.broadcasted_iota(jnp.int32, a.shape, 0)
    s = 1
    while s < rows:
        if reverse:
            a_sh = pltpu.roll(a, rows - s, 0)
            u_sh = pltpu.roll(u, rows - s, 0)
            ok = row < rows - s
        else:
            a_sh = pltpu.roll(a, s, 0)
            u_sh = pltpu.roll(u, s, 0)
            ok = row >= s
        u = jnp.where(ok, a * u_sh + u, u)
        a = jnp.where(ok, a * a_sh, a)
        s *= 2
    return a, u


def _rglru_kernel(x_ref, y_ref, h0_ref, wc_ref, bc_ref, wg_ref, bg_ref, lam_ref,
                  o_ref, hl_ref, xpad, af, uf, ab, ub, *, rows):
    n, cw = x_ref.shape
    nchunk = n // rows
    pad = SUBLANES
    taps = wc_ref.shape[0]
    left = (taps - 1) // 2

    xpad[0:pad, :] = jnp.zeros((pad, cw), F32)
    xpad[pad:pad + n, :] = x_ref[...]
    xpad[pad + n:pad + n + pad, :] = jnp.zeros((pad, cw), F32)

    lam = lam_ref[...]
    softplus = jnp.maximum(-lam, 0.0) + jnp.log1p(jnp.exp(-jnp.abs(lam)))
    wc = wc_ref[...]
    bc = bc_ref[...]
    wg = wg_ref[...]
    bg = bg_ref[...]

    def gates(ci, carry):
        r0 = pl.multiple_of(ci * rows, rows)
        win = xpad[pl.ds(r0, rows + 2 * pad), :]
        xr = jnp.zeros((rows, cw), F32) + bc
        for t in range(taps):
            xr = xr + _row_shift(win, pad + t - left, rows) * wc[t:t + 1]
        g = jnp.dot(xr.astype(BF16), wg, preferred_element_type=F32) + bg
        for d, (a_ref, u_ref) in enumerate(((af, uf), (ab, ub))):
            r = _sigmoid(g[:, (2 * d) * cw:(2 * d + 1) * cw])
            i = _sigmoid(g[:, (2 * d + 1) * cw:(2 * d + 2) * cw])
            a = jnp.exp(-LRU_C * r * softplus[:, d * cw:(d + 1) * cw])
            u = jnp.sqrt(jnp.maximum(1.0 - a * a, 0.0)) * (i * xr)
            a_ref[pl.ds(r0, rows), :] = a
            u_ref[pl.ds(r0, rows), :] = u
        return carry

    lax.fori_loop(0, nchunk, gates, 0)

    def forward(ci, h):
        r0 = pl.multiple_of(ci * rows, rows)
        p, hz = _scan_block(af[pl.ds(r0, rows), :], uf[pl.ds(r0, rows), :], False)
        hs = hz + p * h
        uf[pl.ds(r0, rows), :] = hs
        return hs[rows - 1:rows]

    h_f = lax.fori_loop(0, nchunk, forward, h0_ref[0:1, :])

    def backward(cj, h):
        r0 = pl.multiple_of((nchunk - 1 - cj) * rows, rows)
        p, hz = _scan_block(ab[pl.ds(r0, rows), :], ub[pl.ds(r0, rows), :], True)
        hs = hz + p * h
        yr = y_ref[pl.ds(r0, rows), :]
        o_ref[pl.ds(r0, rows), :] = ((uf[pl.ds(r0, rows), :] + hs) * _gelu_tanh(yr)).astype(o_ref.dtype)
        return hs[0:1]

    h_b = lax.fori_loop(0, nchunk, backward, h0_ref[1:2, :])
    hl_ref[0:1, :] = h_f
    hl_ref[1:2, :] = h_b


def _rglru(z, h0, layer, w_conv4, b_conv4, wg_bd, bg_bd, lam_bd):
    b, n, zc = z.shape
    dm = w_conv4.shape[-1]
    cw = LANES
    nck = dm // cw
    taps = w_conv4.shape[1]
    rows = _block(n, 256)
    xcol = 5 * nck
    ycol = 6 * nck
    seq = pltpu.VMEM((n, cw), F32)
    return pl.pallas_call(
        functools.partial(_rglru_kernel, rows=rows),
        grid=(b, nck),
        in_specs=[
            pl.BlockSpec((None, n, cw), lambda bi, c: (bi, 0, xcol + c)),
            pl.BlockSpec((None, n, cw), lambda bi, c: (bi, 0, ycol + c)),
            pl.BlockSpec((None, 2, cw), lambda bi, c: (bi, 0, c)),
            pl.BlockSpec((None, taps, cw), lambda bi, c: (layer, 0, c)),
            pl.BlockSpec((None, 1, cw), lambda bi, c: (layer, 0, c)),
            pl.BlockSpec((None, None, cw, 4 * cw), lambda bi, c: (layer, c, 0, 0)),
            pl.BlockSpec((None, None, 1, 4 * cw), lambda bi, c: (layer, c, 0, 0)),
            pl.BlockSpec((None, None, 1, 2 * cw), lambda bi, c: (layer, c, 0, 0)),
        ],
        out_specs=[
            pl.BlockSpec((None, n, cw), lambda bi, c: (bi, 0, c)),
            pl.BlockSpec((None, 2, cw), lambda bi, c: (bi, 0, c)),
        ],
        out_shape=[
            jax.ShapeDtypeStruct((b, n, dm), BF16),
            jax.ShapeDtypeStruct((b, 2, dm), F32),
        ],
        scratch_shapes=[pltpu.VMEM((n + 2 * SUBLANES, cw), F32), seq, seq, seq, seq],
        compiler_params=_params("arbitrary", "arbitrary"),
        name="rglru",
    )(z, z, h0, w_conv4, b_conv4, wg_bd, bg_bd, lam_bd)


def _merge_kernel(x_ref, g_ref, sc_ref, sh_ref, ya, yb, yc, yd, g0, g1, g2, g3, b0, b1, b2, b3,
                  o_ref, h_scr):
    bm = x_ref.shape[0]
    j = pl.program_id(1)

    def merge_rows(h, r0, rows):
        acc = None
        for y_ref, wg_ref, wb_ref in ((ya, g0, b0), (yb, g1, b1), (yc, g2, b2), (yd, g3, b3)):
            gate = _sigmoid(jnp.dot(h, wg_ref[...], preferred_element_type=F32))
            term = gate * jnp.dot(y_ref[r0:r0 + rows, :], wb_ref[...], preferred_element_type=F32)
            acc = term if acc is None else acc + term
        o_ref[r0:r0 + rows, :] = acc.astype(o_ref.dtype)

    @pl.when(j == 0)
    def _():
        rc = _block(bm, NORM_CHUNK)
        for r0 in range(0, bm, rc):
            h = _adaln(x_ref[r0:r0 + rc, :], g_ref[...], sc_ref[...], sh_ref[...])
            h_scr[r0:r0 + rc, :] = h
            merge_rows(h, r0, rc)

    @pl.when(j > 0)
    def _():
        merge_rows(h_scr[...], 0, bm)


def _merge(x, mod, layer, rows_per_mod, base_row, g_norm, ys, w_gate, w_branch):
    t, d = x.shape
    dm = d // 4
    bm, row_fn = _row_blocks(rows_per_mod, base_row)
    bn = _block(d, 256)
    ncb = d // bn
    in_specs = [
        pl.BlockSpec((bm, d), lambda i, j: (i, 0)),
        pl.BlockSpec((None, 1, d), lambda i, j: (layer, 0, 0)),
        _mod_spec(layer, 1, row_fn, d),
        _mod_spec(layer, 0, row_fn, d),
    ]
    in_specs += [pl.BlockSpec((bm, dm), lambda i, j: (i, 0))] * 4
    in_specs += [pl.BlockSpec((None, d, bn), functools.partial(lambda i, j, jb: (layer, 0, jb * ncb + j), jb=jb))
                 for jb in range(4)]
    in_specs += [pl.BlockSpec((None, None, dm, bn), functools.partial(lambda i, j, jb: (layer, jb, 0, j), jb=jb))
                 for jb in range(4)]
    return pl.pallas_call(
        _merge_kernel,
        grid=(t // bm, ncb),
        in_specs=in_specs,
        out_specs=pl.BlockSpec((bm, bn), lambda i, j: (i, j)),
        out_shape=jax.ShapeDtypeStruct((t, d), BF16),
        scratch_shapes=[pltpu.VMEM((bm, d), BF16)],
        compiler_params=_params("arbitrary", "arbitrary"),
        name="gated_merge",
    )(x, g_norm, mod, mod, *ys, *([w_gate] * 4), *([w_branch] * 4))


def _resid_kernel(a_ref, w_ref, x_ref, gt_ref, o_ref):
    o_ref[...] = x_ref[...] + gt_ref[...] * jnp.dot(a_ref[...], w_ref[...], preferred_element_type=F32)


def _residual_proj(a, w, x, mod, layer, which, rows_per_mod, base_row, name, row_pref=ROW_BLOCK, col_pref=512):
    t, kdim = a.shape
    d = x.shape[1]
    bm, row_fn = _row_blocks(rows_per_mod, base_row, row_pref)
    bn = _block(d, col_pref)
    return pl.pallas_call(
        _resid_kernel,
        grid=(t // bm, d // bn),
        in_specs=[
            pl.BlockSpec((bm, kdim), lambda i, j: (i, 0)),
            pl.BlockSpec((None, kdim, bn), lambda i, j: (layer, 0, j)),
            pl.BlockSpec((bm, bn), lambda i, j: (i, j)),
            _mod_spec(layer, which, row_fn, bn, col=True),
        ],
        out_specs=pl.BlockSpec((bm, bn), lambda i, j: (i, j)),
        out_shape=jax.ShapeDtypeStruct((t, d), F32),
        compiler_params=_params("arbitrary", "arbitrary"),
        name=name,
    )(a, w, x, mod)


def _ffn_up_kernel(p_ref, m_ref, n_ref, g_ref, sc_ref, sh_ref, wa_ref, wu_ref, wc_ref, bc_ref,
                   o_ref, h_scr, *, seq):
    bm = m_ref.shape[0]
    halo = p_ref.shape[0]
    i = pl.program_id(0)
    j = pl.program_id(1)

    def norm_rows(ref, r0, rows, lo):
        h_scr[lo:lo + rows, :] = _adaln(ref[r0:r0 + rows, :], g_ref[...], sc_ref[...], sh_ref[...])

    def ffn_rows(r0, rows):
        h = h_scr[r0:r0 + rows + 2 * halo, :]
        a = jnp.dot(h, wa_ref[...], preferred_element_type=F32)
        u = jnp.dot(h[halo:halo + rows], wu_ref[...], preferred_element_type=F32)
        tok = i * bm + r0 + lax.broadcasted_iota(jnp.int32, u.shape, 0)
        pos = (tok & (seq - 1)) if seq & (seq - 1) == 0 else lax.rem(tok, seq)
        w = wc_ref[...]
        taps = w.shape[0]
        left = (taps - 1) // 2
        acc = jnp.zeros(u.shape, F32) + bc_ref[...]
        for t in range(taps):
            delta = t - left
            tap = _row_shift(a, halo + delta, rows) * w[t:t + 1]
            if delta < 0:
                tap = jnp.where(pos >= -delta, tap, 0.0)
            elif delta > 0:
                tap = jnp.where(pos < seq - delta, tap, 0.0)
            acc = acc + tap
        o_ref[r0:r0 + rows, :] = (_silu(acc) * u).astype(o_ref.dtype)

    @pl.when(j == 0)
    def _():
        rc = _block(bm, NORM_CHUNK)
        norm_rows(p_ref, 0, halo, 0)
        norm_rows(m_ref, 0, rc, halo)
        for r0 in range(0, bm, rc):
            if r0 + rc < bm:
                norm_rows(m_ref, r0 + rc, rc, halo + r0 + rc)
            else:
                norm_rows(n_ref, 0, halo, halo + bm)
            ffn_rows(r0, rc)

    @pl.when(j > 0)
    def _():
        ffn_rows(0, bm)


def _ffn_up(x, mod, layer, rows_per_mod, base_row, g_norm, wa, wu, w_conv, b_conv, seq):
    t, d = x.shape
    ffp = wa.shape[-1]
    taps = w_conv.shape[1]
    halo = 16
    bm, row_fn = _row_blocks(rows_per_mod, base_row)
    bn = _block(ffp, 512)
    per = bm // halo
    nhalo = t // halo
    return pl.pallas_call(
        functools.partial(_ffn_up_kernel, seq=seq),
        grid=(t // bm, ffp // bn),
        in_specs=[
            pl.BlockSpec((halo, d), lambda i, j: (jnp.maximum(i * per - 1, 0), 0)),
            pl.BlockSpec((bm, d), lambda i, j: (i, 0)),
            pl.BlockSpec((halo, d), lambda i, j: (jnp.minimum((i + 1) * per, nhalo - 1), 0)),
            pl.BlockSpec((None, 1, d), lambda i, j: (layer, 0, 0)),
            _mod_spec(layer, 4, row_fn, d),
            _mod_spec(layer, 3, row_fn, d),
            pl.BlockSpec((None, d, bn), lambda i, j: (layer, 0, j)),
            pl.BlockSpec((None, d, bn), lambda i, j: (layer, 0, j)),
            pl.BlockSpec((None, taps, bn), lambda i, j: (layer, 0, j)),
            pl.BlockSpec((None, 1, bn), lambda i, j: (layer, 0, j)),
        ],
        out_specs=pl.BlockSpec((bm, bn), lambda i, j: (i, j)),
        out_shape=jax.ShapeDtypeStruct((t, ffp), BF16),
        scratch_shapes=[pltpu.VMEM((bm + 2 * halo, d), BF16)],
        compiler_params=_params("arbitrary", "arbitrary"),
        name="ffn_up",
    )(x, x, x, g_norm, mod, mod, wa, wu, w_conv, b_conv)


def _final_norm_kernel(x_ref, g_ref, o_ref):
    x = x_ref[...]
    o_ref[...] = x * lax.rsqrt(jnp.mean(x * x, axis=-1, keepdims=True) + EPS) * g_ref[...]


def _final_norm(x, g):
    t, d = x.shape
    bm = _block(t, 512)
    return pl.pallas_call(
        _final_norm_kernel,
        grid=(t // bm,),
        in_specs=[pl.BlockSpec((bm, d), lambda i: (i, 0)), pl.BlockSpec((1, d), lambda i: (0, 0))],
        out_specs=pl.BlockSpec((bm, d), lambda i: (i, 0)),
        out_shape=jax.ShapeDtypeStruct((t, d), F32),
        compiler_params=_params("arbitrary"),
        name="final_norm",
    )(x, g.reshape(1, d))


def _rope_tables(n):
    pairs = DH // 4
    rows = n // GRID_W
    t_row = jnp.repeat(jnp.arange(rows), GRID_W).astype(F32)
    t_col = jnp.tile(jnp.arange(GRID_W), rows).astype(F32)
    inv = jnp.power(ROPE_BASE, -jnp.arange(pairs, dtype=F32) / pairs)
    ang_row = t_row[:, None] * inv
    ang_col = t_col[:, None] * inv
    ang = jnp.concatenate([ang_row, ang_row, ang_col, ang_col], axis=1)
    sign = np.tile(np.repeat(np.array([-1.0, 1.0], np.float32), pairs), 2)
    cos = jnp.cos(ang)
    sin = jnp.sin(ang) * sign
    reps = LANES // DH
    return jnp.tile(cos, (1, reps)), jnp.tile(sin, (1, reps))


def _block_diag_gates(w_rg_a, b_rg_a, w_rg_x, b_rg_x, lru_lambda, dm):
    depth = w_rg_a.shape[0]
    cw = LANES
    nck = dm // cw
    per = cw // BS_R

    def dense(w):
        w = w.reshape(depth, 2, nck, per, BS_R, BS_R)
        eye = jnp.eye(per, dtype=w.dtype)
        full = jnp.einsum('ldcpij,pq->ldcpiqj', w, eye)
        return full.reshape(depth, 2, nck, cw, cw)

    wa, wx = dense(w_rg_a), dense(w_rg_x)
    wg = jnp.concatenate([wa[:, 0], wx[:, 0], wa[:, 1], wx[:, 1]], axis=-1).astype(BF16)

    def vec(v):
        return v.reshape(depth, 2, nck, 1, cw)

    ba, bx = vec(b_rg_a), vec(b_rg_x)
    bg = jnp.concatenate([ba[:, 0], bx[:, 0], ba[:, 1], bx[:, 1]], axis=-1)
    lam = vec(lru_lambda)
    lam = jnp.concatenate([lam[:, 0], lam[:, 1]], axis=-1)
    return wg, bg, lam


def kernel(x_prompt, x_sample, cache_k, cache_v, state_lru, c, c_ctx, w_mod, b_mod, g_norm1, g_norm2, g_final, w_in, lam_q1, lam_k1, lam_q2, lam_k2, g_subln, w_dw31, b_dw31, g_ln_conv, b_ln_conv, w_dw3, w_conv4, b_conv4, w_rg_a, b_rg_a, w_rg_x, b_rg_x, lru_lambda, w_branch, w_out, w_ffn_up, w_ffn_conv, b_ffn_conv, w_ffn_down):
    depth, d, _ = w_in.shape
    dm = d // 4
    bc, sc, _ = x_prompt.shape
    bd, sd, _ = x_sample.shape
    past = cache_k.shape[2]
    d_ff = w_ffn_conv.shape[-1]
    ffp = -(-d_ff // 512) * 512

    rows = -(-(bd + 1) // SUBLANES) * SUBLANES
    cvec = jnp.zeros((rows, d), F32).at[:bd].set(c).at[bd].set(c_ctx)
    mod = _modulation(cvec, w_mod, b_mod).reshape(depth, rows, 6, 1, d)

    qkvz = jnp.concatenate([w_in[:, :, :5 * dm], w_in[:, :, 6 * dm:8 * dm], w_in[:, :, 5 * dm:6 * dm],
                            w_in[:, :, 8 * dm:10 * dm]], axis=-1).astype(BF16)
    w_gate = w_in[:, :, 10 * dm:].astype(BF16)
    w_branch_b = w_branch.astype(BF16)
    w_out_b = w_out.astype(BF16)
    padc = [(0, 0), (0, 0), (0, ffp - d_ff)]
    w_up_a = jnp.pad(w_ffn_up[:, :, :d_ff], padc).astype(BF16)
    w_up_u = jnp.pad(w_ffn_up[:, :, d_ff:], padc).astype(BF16)
    w_fc = jnp.pad(w_ffn_conv, padc)
    b_fc = jnp.pad(b_ffn_conv, [(0, 0), (0, ffp - d_ff)]).reshape(depth, 1, ffp)
    w_down = jnp.pad(w_ffn_down, [(0, 0), (0, ffp - d_ff), (0, 0)]).astype(BF16)
    wg_bd, bg_bd, lam_bd = _block_diag_gates(w_rg_a, b_rg_a, w_rg_x, b_rg_x, lru_lambda, dm)

    def v3(p):
        return p.reshape(depth, 1, p.shape[-1])

    g1, g2 = v3(g_norm1), v3(g_norm2)
    lam_p = [v3(lam_q1), v3(lam_k1), v3(lam_q2), v3(lam_k2)]
    g_sub = v3(g_subln)
    b31, gln, bln, b4 = v3(b_dw31), v3(g_ln_conv), v3(b_ln_conv), v3(b_conv4)

    def layer_pass(x, layer, b, n, rpm, base, tables, ctx_k, ctx_v, h0, want_kv):
        outs = _in_projection(x, mod, layer, rpm, base, g1, qkvz, tables, n, want_kv)
        if want_kv:
            q, k, v, kv32, z = outs
        else:
            q, k, v, z = outs
            kv32 = None
        q, k, v, z = (a.reshape(b, n, -1) for a in (q, k, v, z))
        y_a = _attention(q, k, v, ctx_k, ctx_v, layer, lam_p, g_sub)
        y_b = _conv_module(z, layer, w_dw31, b31, gln, bln)
        y_c = _short_conv(z, layer, w_dw3)
        y_d, h_last = _rglru(z, h0, layer, w_conv4, b4, wg_bd, bg_bd, lam_bd)
        ys = [y.reshape(b * n, dm) for y in (y_a, y_b, y_c, y_d)]
        merged = _merge(x, mod, layer, rpm, base, g1, ys, w_gate, w_branch_b)
        x1 = _residual_proj(merged, w_out_b, x, mod, layer, 2, rpm, base, "out_projection",
                            row_pref=512, col_pref=d)
        act = _ffn_up(x1, mod, layer, rpm, base, g2, w_up_a, w_up_u, w_fc, b_fc, n)
        x2 = _residual_proj(act, w_down, x1, mod, layer, 5, rpm, base, "ffn_down")
        return x2, kv32, h_last

    xp = x_prompt.reshape(bc * sc, d)
    zero_state = jnp.zeros((bc, 2, dm), F32)
    ks, vs, ss = [], [], []
    for layer in range(depth):
        xp, kv32, h_last = layer_pass(xp, layer, bc, sc, bc * sc, bd, None, None, None, zero_state, True)
        kv32 = kv32.reshape(bc, sc, 2 * dm)
        ks.append(kv32[:, :, :dm])
        vs.append(kv32[:, :, dm:])
        ss.append(h_last)
    heads = dm // V_DIM
    y_prompt = _final_norm(xp, g_final).reshape(bc, sc, d)
    new_cache_k = jnp.stack(ks, axis=1).reshape(bc, depth, sc, heads, 2, DH)
    new_cache_v = jnp.stack(vs, axis=1).reshape(bc, depth, sc, heads, V_DIM)
    new_state = jnp.stack(ss, axis=1)

    tables = _rope_tables(sd)
    ck = cache_k.reshape(bd, depth, past, dm).astype(BF16)
    cv = _values_t(cache_v.reshape(bd, depth, past, dm).astype(BF16), dm // V_DIM)
    xs = x_sample.reshape(bd * sd, d)
    for layer in range(depth):
        xs, _, _ = layer_pass(xs, layer, bd, sd, sd, 0, tables, ck, cv, state_lru[:, layer], False)
    y_sample = _final_norm(xs, g_final).reshape(bd, sd, d)
    return (y_prompt, y_sample, new_cache_k, new_cache_v, new_state)
```

```python
import functools
import math

import jax
import jax.numpy as jnp
import numpy as np
from jax import lax
from jax.experimental import pallas as pl
from jax.experimental.pallas import tpu as pltpu

F32 = jnp.float32
BF16 = jnp.bfloat16

EPS = 1e-6
ROPE_BASE = 10000.0
GRID_W = 64
LRU_C = 8.0
V_DIM = 128
DH = V_DIM // 2
V_ROWS = V_DIM + 16
BS_R = 64
LANES = 128
SUBLANES = 8
VMEM_LIMIT_BYTES = 56 * 1024 * 1024
ROW_BLOCK = 1024
KEY_CHUNK = 1024
QUERY_BLOCK = 512
HEADS_PER_STEP = 2
EXP_STRIP = 32
NORM_CHUNK = 256


def _params(*sem):
    return pltpu.CompilerParams(dimension_semantics=sem, vmem_limit_bytes=VMEM_LIMIT_BYTES)


def _block(n, pref):
    b = min(n, pref)
    while n % b:
        b //= 2
    return b


def _row_blocks(rows_per_mod, base_row, pref=ROW_BLOCK):
    bm = _block(rows_per_mod, pref)
    return bm, (lambda i: base_row + (i * bm) // rows_per_mod)


def _sigmoid(x):
    return 1.0 / (1.0 + jnp.exp(-x))


def _silu(x):
    return x * _sigmoid(x)


def _gelu_tanh(x):
    return 0.5 * x * (1.0 + jnp.tanh(math.sqrt(2.0 / math.pi) * (x + 0.044715 * (x * x * x))))


def _adaln(x, g, sc, sh):
    y = x * lax.rsqrt(jnp.mean(x * x, axis=-1, keepdims=True) + EPS) * g
    return (y * (1.0 + sc) + sh).astype(BF16)


def _row_shift(win, off, rows):
    n = win.shape[0]
    sub = off % SUBLANES
    base = off - sub
    if sub:
        win = pltpu.roll(win, n - sub, 0)
    return win[base:base + rows]


def _mod_kernel(c_ref, w_ref, b_ref, o_ref):
    c = c_ref[...]
    s = _silu(c).astype(BF16)
    o_ref[...] = jnp.dot(s, w_ref[...].astype(BF16), preferred_element_type=F32) + b_ref[...]


def _modulation(cvec, w_mod, b_mod):
    depth, d, n6 = w_mod.shape
    rows = cvec.shape[0]
    bn = _block(n6, 1024)
    return pl.pallas_call(
        _mod_kernel,
        grid=(depth, n6 // bn),
        in_specs=[
            pl.BlockSpec((rows, d), lambda l, n: (0, 0)),
            pl.BlockSpec((None, d, bn), lambda l, n: (l, 0, n)),
            pl.BlockSpec((None, 1, bn), lambda l, n: (l, 0, n)),
        ],
        out_specs=pl.BlockSpec((None, rows, bn), lambda l, n: (l, 0, n)),
        out_shape=jax.ShapeDtypeStruct((depth, rows, n6), F32),
        compiler_params=_params("arbitrary", "arbitrary"),
        name="modulation",
    )(cvec, w_mod, b_mod.reshape(depth, 1, n6))


def _mod_spec(layer, which, row_fn, bn, col=False):
    def full(i, j):
        return (layer, row_fn(i), which, 0, 0)

    def cols(i, j):
        return (layer, row_fn(i), which, 0, j)

    return pl.BlockSpec((None, None, None, 1, bn), cols if col else full)


def _rope_store(out_ref, r0, acc, cos, sin, scale):
    rows = acc.shape[0]
    lane = lax.broadcasted_iota(jnp.int32, (rows, LANES), 1)
    first_half = (lane % 32) < 16
    for c in range(acc.shape[1] // LANES):
        a = acc[:, c * LANES:(c + 1) * LANES]
        up = pltpu.roll(a, LANES - 16, 1)
        down = pltpu.roll(a, 16, 1)
        r = a * cos + jnp.where(first_half, up, down) * sin
        if scale != 1.0:
            r = r * scale
        out_ref[r0:r0 + rows, c * LANES:(c + 1) * LANES] = r.astype(out_ref.dtype)


def _in_kernel(*refs, rope, kv32):
    x_ref, g_ref, sc_ref, sh_ref, w_ref = refs[:5]
    pos = 5
    if rope:
        cos_ref, sin_ref = refs[pos:pos + 2]
        pos += 2
    q_out, k_out, v_out = refs[pos:pos + 3]
    pos += 3
    if kv32:
        kv_out = refs[pos]
        pos += 1
    z_out, h_scr = refs[pos:pos + 2]

    j = pl.program_id(1)
    q_scale = DH ** -0.5 * math.log2(math.e)

    @pl.when(j == 0)
    def _():
        bm = x_ref.shape[0]
        rc = _block(bm, NORM_CHUNK)
        for r0 in range(0, bm, rc):
            h = _adaln(x_ref[r0:r0 + rc, :], g_ref[...], sc_ref[...], sh_ref[...])
            h_scr[r0:r0 + rc, :] = h
            acc = jnp.dot(h, w_ref[...], preferred_element_type=F32)
            if rope:
                _rope_store(q_out, r0, acc, cos_ref[r0:r0 + rc, :], sin_ref[r0:r0 + rc, :], q_scale)
            else:
                q_out[r0:r0 + rc, :] = (acc * q_scale).astype(BF16)

    @pl.when(j == 1)
    def _():
        bm = x_ref.shape[0]
        rc = _block(bm, NORM_CHUNK) if rope else bm
        for r0 in range(0, bm, rc):
            acc = jnp.dot(h_scr[r0:r0 + rc, :], w_ref[...], preferred_element_type=F32)
            if rope:
                _rope_store(k_out, r0, acc, cos_ref[r0:r0 + rc, :], sin_ref[r0:r0 + rc, :], 1.0)
            else:
                k_out[r0:r0 + rc, :] = acc.astype(BF16)
            if kv32:
                kv_out[r0:r0 + rc, :] = acc

    @pl.when(j > 1)
    def _():
        acc = jnp.dot(h_scr[...], w_ref[...], preferred_element_type=F32)

        @pl.when(j == 2)
        def _():
            v_out[...] = acc.astype(BF16)
            if kv32:
                kv_out[...] = acc

        @pl.when(j >= 3)
        def _():
            z_out[...] = acc


def _in_projection(x, mod, layer, rows_per_mod, base_row, g_norm, w_in, tables, seq, kv32):
    t, d = x.shape
    dm = d // 4
    ncol = w_in.shape[2]
    nz = ncol // dm - 3
    bm, row_fn = _row_blocks(rows_per_mod, base_row)
    rope = tables is not None
    in_specs = [
        pl.BlockSpec((bm, d), lambda i, j: (i, 0)),
        pl.BlockSpec((None, 1, d), lambda i, j: (layer, 0, 0)),
        _mod_spec(layer, 1, row_fn, d),
        _mod_spec(layer, 0, row_fn, d),
        pl.BlockSpec((None, d, dm), lambda i, j: (layer, 0, j)),
    ]
    args = [x, g_norm, mod, mod, w_in]
    if rope:
        nblk_seq = seq // bm
        in_specs += [pl.BlockSpec((bm, LANES), lambda i, j: (i % nblk_seq, 0))] * 2
        args += list(tables)
    out_specs = [pl.BlockSpec((bm, dm), lambda i, j: (i, 0))] * 3
    out_shape = [jax.ShapeDtypeStruct((t, dm), BF16)] * 3
    if kv32:
        out_specs.append(pl.BlockSpec((bm, dm), lambda i, j: (i, jnp.clip(j - 1, 0, 1))))
        out_shape.append(jax.ShapeDtypeStruct((t, 2 * dm), F32))
    out_specs.append(pl.BlockSpec((bm, dm), lambda i, j: (i, jnp.clip(j - 3, 0, nz - 1))))
    out_shape.append(jax.ShapeDtypeStruct((t, nz * dm), F32))
    return pl.pallas_call(
        functools.partial(_in_kernel, rope=rope, kv32=kv32),
        grid=(t // bm, ncol // dm),
        in_specs=in_specs,
        out_specs=out_specs,
        out_shape=out_shape,
        scratch_shapes=[pltpu.VMEM((bm, d), BF16)],
        compiler_params=_params("arbitrary", "arbitrary"),
        name="in_projection",
    )(*args)


def _attn_kernel(*refs, has_ctx, lam_init, kc, hp):
    q_ref, k_ref, vt_ref = refs[:3]
    pos = 3
    if has_ctx:
        ck_ref, cv_ref = refs[pos:pos + 2]
        pos += 2
    lq1, lk1, lq2, lk2, g_ref, o_ref, e_scr = refs[pos:pos + 7]

    lam = (jnp.exp(jnp.sum(lq1[...] * lk1[...], axis=-1, keepdims=True))
           - jnp.exp(jnp.sum(lq2[...] * lk2[...], axis=-1, keepdims=True)) + lam_init)

    bq = q_ref.shape[0]
    nt = (((1,), (1,)), ((), ()))

    def head(hh):
        return slice(hh * V_DIM, (hh + 1) * V_DIM)

    qs = []
    for hh in range(hp):
        q = q_ref[:, head(hh)]
        lane = lax.broadcasted_iota(jnp.int32, q.shape, 1)
        zero = jnp.zeros_like(q)
        qs.append(jnp.concatenate([jnp.where(lane < DH, q, zero), jnp.where(lane >= DH, q, zero)], axis=0))

    chunks = []
    if has_ctx:
        chunks.append((lambda hh: ck_ref[:, head(hh)], lambda hh: cv_ref[hh]))
    for c in range(k_ref.shape[0] // kc):
        rows = slice(c * kc, (c + 1) * kc)
        chunks.append((lambda hh, rows=rows: k_ref[rows, head(hh)], lambda hh, rows=rows: vt_ref[hh, :, rows]))

    def scores(ci, hh):
        return lax.dot_general(chunks[ci][0](hh), qs[hh], nt, preferred_element_type=F32)

    def exp_rows(s, m_cur, e_view):
        rows = s.shape[0]
        strip = _block(rows, EXP_STRIP)
        for r0 in range(0, rows, strip):
            e_view[r0:r0 + strip, :] = jnp.exp2(s[r0:r0 + strip] - m_cur).astype(BF16)
        return e_view[0:rows, :]

    m = [None] * hp
    acc = [None] * hp
    s_next = [scores(0, hh) for hh in range(hp)]
    for ci, (_, load_vt) in enumerate(chunks):
        for hh in range(hp):
            s = s_next[hh]
            if ci + 1 < len(chunks):
                s_next[hh] = scores(ci + 1, hh)
            smax = jnp.max(s, axis=0, keepdims=True)
            e_view = e_scr.at[ci % 2, hh]
            if m[hh] is None:
                m[hh] = smax
                e = exp_rows(s, smax, e_view)
                acc[hh] = jnp.dot(load_vt(hh), e, preferred_element_type=F32)
            else:
                m_new = jnp.maximum(m[hh], smax)
                alpha = jnp.exp2(m[hh] - m_new)
                e = exp_rows(s, m_new, e_view)
                acc[hh] = alpha * acc[hh] + jnp.dot(load_vt(hh), e, preferred_element_type=F32)
                m[hh] = m_new

    for hh in range(hp):
        inv = 1.0 / acc[hh][V_DIM:V_DIM + 1]
        val = acc[hh][:V_DIM]
        o = (val[:, :bq] * inv[:, :bq] - val[:, bq:] * (lam * inv[:, bq:])).T
        y = o * lax.rsqrt(jnp.mean(o * o, axis=-1, keepdims=True) + EPS) * g_ref[...]
        o_ref[:, head(hh)] = (y * (1.0 - lam_init)).astype(o_ref.dtype)


def _values_t(v, heads):
    lead = v.shape[:-2]
    keys = v.shape[-2]
    vt = jnp.swapaxes(v, -1, -2).reshape(*lead, heads, V_DIM, keys)
    ones = jnp.ones((*lead, heads, V_ROWS - V_DIM, keys), v.dtype)
    return jnp.concatenate([vt, ones], axis=-2)


def _attention(q, k, v, ctx_k, ctx_vt, layer, lam_p, g_subln):
    b, n, dm = q.shape
    heads = dm // V_DIM
    hp = _block(heads, HEADS_PER_STEP)
    hw = hp * V_DIM
    bq = _block(n, QUERY_BLOCK)
    kc = _block(n, KEY_CHUNK)
    has_ctx = ctx_k is not None
    lam_init = 0.8 - 0.6 * math.exp(-0.3 * layer)
    in_specs = [
        pl.BlockSpec((None, bq, hw), lambda bi, h, i: (bi, i, h)),
        pl.BlockSpec((None, n, hw), lambda bi, h, i: (bi, 0, h)),
        pl.BlockSpec((None, hp, V_ROWS, n), lambda bi, h, i: (bi, h, 0, 0)),
    ]
    args = [q, k, _values_t(v, heads)]
    if has_ctx:
        past = ctx_k.shape[2]
        in_specs += [pl.BlockSpec((None, None, past, hw), lambda bi, h, i: (bi, layer, 0, h)),
                     pl.BlockSpec((None, None, hp, V_ROWS, past), lambda bi, h, i: (bi, layer, h, 0, 0))]
        args += [ctx_k, ctx_vt]
    in_specs += [pl.BlockSpec((None, 1, DH), lambda bi, h, i: (layer, 0, 0))] * 4
    in_specs += [pl.BlockSpec((None, 1, V_DIM), lambda bi, h, i: (layer, 0, 0))]
    args += list(lam_p) + [g_subln]
    return pl.pallas_call(
        functools.partial(_attn_kernel, has_ctx=has_ctx, lam_init=lam_init, kc=kc, hp=hp),
        grid=(b, heads // hp, n // bq),
        in_specs=in_specs,
        out_specs=pl.BlockSpec((None, bq, hw), lambda bi, h, i: (bi, i, h)),
        out_shape=jax.ShapeDtypeStruct((b, n, dm), BF16),
        scratch_shapes=[pltpu.VMEM((2, hp, max(kc, past if has_ctx else 0), 2 * bq), BF16)],
        compiler_params=_params("arbitrary", "arbitrary", "arbitrary"),
        name="diff_attention",
    )(*args)


def _halo_specs(bn, halo, width, col, nseq_halo):
    per = bn // halo
    return [
        pl.BlockSpec((None, halo, width), lambda b, i: (b, jnp.maximum(i * per - 1, 0), col)),
        pl.BlockSpec((None, bn, width), lambda b, i: (b, i, col)),
        pl.BlockSpec((None, halo, width), lambda b, i: (b, jnp.minimum((i + 1) * per, nseq_halo - 1), col)),
    ]


def _edge_masks():
    i = pl.program_id(1)
    return i > 0, i < pl.num_programs(1) - 1


def _convmod_kernel(p_ref, m_ref, n_ref, w_ref, b_ref, g_ref, bl_ref, o_ref, *, halo):
    dm = o_ref.shape[-1]
    bn = o_ref.shape[0]
    taps = w_ref.shape[0]
    has_prev, has_next = _edge_masks()

    def glu(ref):
        z = ref[...]
        return z[:, :dm] * _sigmoid(z[:, dm:])

    prev = jnp.where(has_prev, glu(p_ref), 0.0)
    nxt = jnp.where(has_next, glu(n_ref), 0.0)
    win = jnp.concatenate([prev, glu(m_ref), nxt], axis=0)
    w = w_ref[...]
    left = (taps - 1) // 2
    acc = jnp.zeros((bn, dm), F32) + b_ref[...]
    shifted = {}
    for t in range(taps):
        off = halo + t - left
        sub = off % SUBLANES
        if sub not in shifted:
            shifted[sub] = pltpu.roll(win, win.shape[0] - sub, 0) if sub else win
        base = off - sub
        acc = acc + shifted[sub][base:base + bn] * w[t:t + 1]
    mu = jnp.mean(acc, axis=-1, keepdims=True)
    xc = acc - mu
    y = xc * lax.rsqrt(jnp.mean(xc * xc, axis=-1, keepdims=True) + EPS) * g_ref[...] + bl_ref[...]
    o_ref[...] = _silu(y).astype(o_ref.dtype)


def _conv_module(z, layer, w_dw, b_dw, g_ln, b_ln):
    b, n, _ = z.shape
    taps, dm = w_dw.shape[1:]
    halo = 16
    bn = _block(n, 512)
    vec = pl.BlockSpec((None, 1, dm), lambda bi, i: (layer, 0, 0))
    return pl.pallas_call(
        functools.partial(_convmod_kernel, halo=halo),
        grid=(b, n // bn),
        in_specs=_halo_specs(bn, halo, 2 * dm, 0, n // halo)
        + [pl.BlockSpec((None, taps, dm), lambda bi, i: (layer, 0, 0)), vec, vec, vec],
        out_specs=pl.BlockSpec((None, bn, dm), lambda bi, i: (bi, i, 0)),
        out_shape=jax.ShapeDtypeStruct((b, n, dm), BF16),
        compiler_params=_params("arbitrary", "arbitrary"),
        name="conformer_conv",
    )(z, z, z, w_dw, b_dw, g_ln, b_ln)


def _shortconv_kernel(p_ref, m_ref, n_ref, gb_ref, w_ref, o_ref, *, halo):
    dm = o_ref.shape[-1]
    bn = o_ref.shape[0]
    taps = w_ref.shape[0]
    has_prev, has_next = _edge_masks()

    def prod(ref):
        z = ref[...]
        return z[:, :dm] * z[:, dm:]

    prev = jnp.where(has_prev, prod(p_ref), 0.0)
    nxt = jnp.where(has_next, prod(n_ref), 0.0)
    win = jnp.concatenate([prev, prod(m_ref), nxt], axis=0)
    w = w_ref[...]
    left = (taps - 1) // 2
    acc = jnp.zeros((bn, dm), F32)
    for t in range(taps):
        acc = acc + _row_shift(win, halo + t - left, bn) * w[t:t + 1]
    o_ref[...] = (gb_ref[...] * acc).astype(o_ref.dtype)


def _short_conv(z, layer, w_dw):
    b, n, _ = z.shape
    taps, dm = w_dw.shape[1:]
    halo = SUBLANES
    bn = _block(n, 512)
    return pl.pallas_call(
        functools.partial(_shortconv_kernel, halo=halo),
        grid=(b, n // bn),
        in_specs=_halo_specs(bn, halo, 2 * dm, 1, n // halo)
        + [pl.BlockSpec((None, bn, dm), lambda bi, i: (bi, i, 4)),
           pl.BlockSpec((None, taps, dm), lambda bi, i: (layer, 0, 0))],
        out_specs=pl.BlockSpec((None, bn, dm), lambda bi, i: (bi, i, 0)),
        out_shape=jax.ShapeDtypeStruct((b, n, dm), BF16),
        compiler_params=_params("arbitrary", "arbitrary"),
        name="short_conv",
    )(z, z, z, z, w_dw)


def _scan_block(a, u, reverse):
    rows = a.shape[0]
    row = lax.broadcasted_iota(jnp.int32, a.shape, 0)
    s = 1
    while s < rows:
        if reverse:
            a_sh = pltpu.roll(a, rows - s, 0)
            u_sh = pltpu.roll(u, rows - s, 0)
            ok = row < rows - s
        else:
            a_sh = pltpu.roll(a, s, 0)
            u_sh = pltpu.roll(u, s, 0)
            ok = row >= s
        u = jnp.where(ok, a * u_sh + u, u)
        a = jnp.where(ok, a * a_sh, a)
        s *= 2
    return a, u


def _rglru_kernel(x_ref, y_ref, h0_ref, wc_ref, bc_ref, wg_ref, bg_ref, lam_ref,
                  o_ref, hl_ref, xpad, af, uf, ab, ub, *, rows):
    n, cw = x_ref.shape
    nchunk = n // rows
    pad = SUBLANES
    taps = wc_ref.shape[0]
    left = (taps - 1) // 2

    xpad[0:pad, :] = jnp.zeros((pad, cw), F32)
    xpad[pad:pad + n, :] = x_ref[...]
    xpad[pad + n:pad + n + pad, :] = jnp.zeros((pad, cw), F32)

    lam = lam_ref[...]
    softplus = jnp.maximum(-lam, 0.0) + jnp.log1p(jnp.exp(-jnp.abs(lam)))
    wc = wc_ref[...]
    bc = bc_ref[...]
    wg = wg_ref[...]
    bg = bg_ref[...]

    def gates(ci, carry):
        r0 = pl.multiple_of(ci * rows, rows)
        win = xpad[pl.ds(r0, rows + 2 * pad), :]
        xr = jnp.zeros((rows, cw), F32) + bc
        for t in range(taps):
            xr = xr + _row_shift(win, pad + t - left, rows) * wc[t:t + 1]
        g = jnp.dot(xr.astype(BF16), wg, preferred_element_type=F32) + bg
        for d, (a_ref, u_ref) in enumerate(((af, uf), (ab, ub))):
            r = _sigmoid(g[:, (2 * d) * cw:(2 * d + 1) * cw])
            i = _sigmoid(g[:, (2 * d + 1) * cw:(2 * d + 2) * cw])
            a = jnp.exp(-LRU_C * r * softplus[:, d * cw:(d + 1) * cw])
            u = jnp.sqrt(jnp.maximum(1.0 - a * a, 0.0)) * (i * xr)
            a_ref[pl.ds(r0, rows), :] = a
            u_ref[pl.ds(r0, rows), :] = u
        return carry

    lax.fori_loop(0, nchunk, gates, 0)

    def forward(ci, h):
        r0 = pl.multiple_of(ci * rows, rows)
        p, hz = _scan_block(af[pl.ds(r0, rows), :], uf[pl.ds(r0, rows), :], False)
        hs = hz + p * h
        uf[pl.ds(r0, rows), :] = hs
        return hs[rows - 1:rows]

    h_f = lax.fori_loop(0, nchunk, forward, h0_ref[0:1, :])

    def backward(cj, h):
        r0 = pl.multiple_of((nchunk - 1 - cj) * rows, rows)
        p, hz = _scan_block(ab[pl.ds(r0, rows), :], ub[pl.ds(r0, rows), :], True)
        hs = hz + p * h
        yr = y_ref[pl.ds(r0, rows), :]
        o_ref[pl.ds(r0, rows), :] = ((uf[pl.ds(r0, rows), :] + hs) * _gelu_tanh(yr)).astype(o_ref.dtype)
        return hs[0:1]

    h_b = lax.fori_loop(0, nchunk, backward, h0_ref[1:2, :])
    hl_ref[0:1, :] = h_f
    hl_ref[1:2, :] = h_b


def _rglru(z, h0, layer, w_conv4, b_conv4, wg_bd, bg_bd, lam_bd):
    b, n, zc = z.shape
    dm = w_conv4.shape[-1]
    cw = LANES
    nck = dm // cw
    taps = w_conv4.shape[1]
    rows = _block(n, 256)
    xcol = 5 * nck
    ycol = 6 * nck
    seq = pltpu.VMEM((n, cw), F32)
    return pl.pallas_call(
        functools.partial(_rglru_kernel, rows=rows),
        grid=(b, nck),
        in_specs=[
            pl.BlockSpec((None, n, cw), lambda bi, c: (bi, 0, xcol + c)),
            pl.BlockSpec((None, n, cw), lambda bi, c: (bi, 0, ycol + c)),
            pl.BlockSpec((None, 2, cw), lambda bi, c: (bi, 0, c)),
            pl.BlockSpec((None, taps, cw), lambda bi, c: (layer, 0, c)),
            pl.BlockSpec((None, 1, cw), lambda bi, c: (layer, 0, c)),
            pl.BlockSpec((None, None, cw, 4 * cw), lambda bi, c: (layer, c, 0, 0)),
            pl.BlockSpec((None, None, 1, 4 * cw), lambda bi, c: (layer, c, 0, 0)),
            pl.BlockSpec((None, None, 1, 2 * cw), lambda bi, c: (layer, c, 0, 0)),
        ],
        out_specs=[
            pl.BlockSpec((None, n, cw), lambda bi, c: (bi, 0, c)),
            pl.BlockSpec((None, 2, cw), lambda bi, c: (bi, 0, c)),
        ],
        out_shape=[
            jax.ShapeDtypeStruct((b, n, dm), BF16),
            jax.ShapeDtypeStruct((b, 2, dm), F32),
        ],
        scratch_shapes=[pltpu.VMEM((n + 2 * SUBLANES, cw), F32), seq, seq, seq, seq],
        compiler_params=_params("arbitrary", "arbitrary"),
        name="rglru",
    )(z, z, h0, w_conv4, b_conv4, wg_bd, bg_bd, lam_bd)


def _merge_kernel(x_ref, g_ref, sc_ref, sh_ref, ya, yb, yc, yd, g0, g1, g2, g3, b0, b1, b2, b3,
                  o_ref, h_scr):
    bm = x_ref.shape[0]
    j = pl.program_id(1)

    def merge_rows(h, r0, rows):
        acc = None
        for y_ref, wg_ref, wb_ref in ((ya, g0, b0), (yb, g1, b1), (yc, g2, b2), (yd, g3, b3)):
            gate = _sigmoid(jnp.dot(h, wg_ref[...], preferred_element_type=F32))
            term = gate * jnp.dot(y_ref[r0:r0 + rows, :], wb_ref[...], preferred_element_type=F32)
            acc = term if acc is None else acc + term
        o_ref[r0:r0 + rows, :] = acc.astype(o_ref.dtype)

    @pl.when(j == 0)
    def _():
        rc = _block(bm, NORM_CHUNK)
        for r0 in range(0, bm, rc):
            h = _adaln(x_ref[r0:r0 + rc, :], g_ref[...], sc_ref[...], sh_ref[...])
            h_scr[r0:r0 + rc, :] = h
            merge_rows(h, r0, rc)

    @pl.when(j > 0)
    def _():
        merge_rows(h_scr[...], 0, bm)


def _merge(x, mod, layer, rows_per_mod, base_row, g_norm, ys, w_gate, w_branch):
    t, d = x.shape
    dm = d // 4
    bm, row_fn = _row_blocks(rows_per_mod, base_row)
    bn = _block(d, 256)
    ncb = d // bn
    in_specs = [
        pl.BlockSpec((bm, d), lambda i, j: (i, 0)),
        pl.BlockSpec((None, 1, d), lambda i, j: (layer, 0, 0)),
        _mod_spec(layer, 1, row_fn, d),
        _mod_spec(layer, 0, row_fn, d),
    ]
    in_specs += [pl.BlockSpec((bm, dm), lambda i, j: (i, 0))] * 4
    in_specs += [pl.BlockSpec((None, d, bn), functools.partial(lambda i, j, jb: (layer, 0, jb * ncb + j), jb=jb))
                 for jb in range(4)]
    in_specs += [pl.BlockSpec((None, None, dm, bn), functools.partial(lambda i, j, jb: (layer, jb, 0, j), jb=jb))
                 for jb in range(4)]
    return pl.pallas_call(
        _merge_kernel,
        grid=(t // bm, ncb),
        in_specs=in_specs,
        out_specs=pl.BlockSpec((bm, bn), lambda i, j: (i, j)),
        out_shape=jax.ShapeDtypeStruct((t, d), BF16),
        scratch_shapes=[pltpu.VMEM((bm, d), BF16)],
        compiler_params=_params("arbitrary", "arbitrary"),
        name="gated_merge",
    )(x, g_norm, mod, mod, *ys, *([w_gate] * 4), *([w_branch] * 4))


def _resid_kernel(a_ref, w_ref, x_ref, gt_ref, o_ref):
    o_ref[...] = x_ref[...] + gt_ref[...] * jnp.dot(a_ref[...], w_ref[...], preferred_element_type=F32)


def _residual_proj(a, w, x, mod, layer, which, rows_per_mod, base_row, name, row_pref=ROW_BLOCK, col_pref=512):
    t, kdim = a.shape
    d = x.shape[1]
    bm, row_fn = _row_blocks(rows_per_mod, base_row, row_pref)
    bn = _block(d, col_pref)
    return pl.pallas_call(
        _resid_kernel,
        grid=(t // bm, d // bn),
        in_specs=[
            pl.BlockSpec((bm, kdim), lambda i, j: (i, 0)),
            pl.BlockSpec((None, kdim, bn), lambda i, j: (layer, 0, j)),
            pl.BlockSpec((bm, bn), lambda i, j: (i, j)),
            _mod_spec(layer, which, row_fn, bn, col=True),
        ],
        out_specs=pl.BlockSpec((bm, bn), lambda i, j: (i, j)),
        out_shape=jax.ShapeDtypeStruct((t, d), F32),
        compiler_params=_params("arbitrary", "arbitrary"),
        name=name,
    )(a, w, x, mod)


def _ffn_up_kernel(p_ref, m_ref, n_ref, g_ref, sc_ref, sh_ref, wa_ref, wu_ref, wc_ref, bc_ref,
                   o_ref, h_scr, *, seq):
    bm = m_ref.shape[0]
    halo = p_ref.shape[0]
    i = pl.program_id(0)
    j = pl.program_id(1)

    def norm_rows(ref, r0, rows, lo):
        h_scr[lo:lo + rows, :] = _adaln(ref[r0:r0 + rows, :], g_ref[...], sc_ref[...], sh_ref[...])

    def ffn_rows(r0, rows):
        h = h_scr[r0:r0 + rows + 2 * halo, :]
        a = jnp.dot(h, wa_ref[...], preferred_element_type=F32)
        u = jnp.dot(h[halo:halo + rows], wu_ref[...], preferred_element_type=F32)
        tok = i * bm + r0 + lax.broadcasted_iota(jnp.int32, u.shape, 0)
        pos = (tok & (seq - 1)) if seq & (seq - 1) == 0 else lax.rem(tok, seq)
        w = wc_ref[...]
        taps = w.shape[0]
        left = (taps - 1) // 2
        acc = jnp.zeros(u.shape, F32) + bc_ref[...]
        for t in range(taps):
            delta = t - left
            tap = _row_shift(a, halo + delta, rows) * w[t:t + 1]
            if delta < 0:
                tap = jnp.where(pos >= -delta, tap, 0.0)
            elif delta > 0:
                tap = jnp.where(pos < seq - delta, tap, 0.0)
            acc = acc + tap
        o_ref[r0:r0 + rows, :] = (_silu(acc) * u).astype(o_ref.dtype)

    @pl.when(j == 0)
    def _():
        rc = _block(bm, NORM_CHUNK)
        norm_rows(p_ref, 0, halo, 0)
        norm_rows(m_ref, 0, rc, halo)
        for r0 in range(0, bm, rc):
            if r0 + rc < bm:
                norm_rows(m_ref, r0 + rc, rc, halo + r0 + rc)
            else:
                norm_rows(n_ref, 0, halo, halo + bm)
            ffn_rows(r0, rc)

    @pl.when(j > 0)
    def _():
        ffn_rows(0, bm)


def _ffn_up(x, mod, layer, rows_per_mod, base_row, g_norm, wa, wu, w_conv, b_conv, seq):
    t, d = x.shape
    ffp = wa.shape[-1]
    taps = w_conv.shape[1]
    halo = 16
    bm, row_fn = _row_blocks(rows_per_mod, base_row)
    bn = _block(ffp, 512)
    per = bm // halo
    nhalo = t // halo
    return pl.pallas_call(
        functools.partial(_ffn_up_kernel, seq=seq),
        grid=(t // bm, ffp // bn),
        in_specs=[
            pl.BlockSpec((halo, d), lambda i, j: (jnp.maximum(i * per - 1, 0), 0)),
            pl.BlockSpec((bm, d), lambda i, j: (i, 0)),
            pl.BlockSpec((halo, d), lambda i, j: (jnp.minimum((i + 1) * per, nhalo - 1), 0)),
            pl.BlockSpec((None, 1, d), lambda i, j: (layer, 0, 0)),
            _mod_spec(layer, 4, row_fn, d),
            _mod_spec(layer, 3, row_fn, d),
            pl.BlockSpec((None, d, bn), lambda i, j: (layer, 0, j)),
            pl.BlockSpec((None, d, bn), lambda i, j: (layer, 0, j)),
            pl.BlockSpec((None, taps, bn), lambda i, j: (layer, 0, j)),
            pl.BlockSpec((None, 1, bn), lambda i, j: (layer, 0, j)),
        ],
        out_specs=pl.BlockSpec((bm, bn), lambda i, j: (i, j)),
        out_shape=jax.ShapeDtypeStruct((t, ffp), BF16),
        scratch_shapes=[pltpu.VMEM((bm + 2 * halo, d), BF16)],
        compiler_params=_params("arbitrary", "arbitrary"),
        name="ffn_up",
    )(x, x, x, g_norm, mod, mod, wa, wu, w_conv, b_conv)


def _final_norm_kernel(x_ref, g_ref, o_ref):
    x = x_ref[...]
    o_ref[...] = x * lax.rsqrt(jnp.mean(x * x, axis=-1, keepdims=True) + EPS) * g_ref[...]


def _final_norm(x, g):
    t, d = x.shape
    bm = _block(t, 512)
    return pl.pallas_call(
        _final_norm_kernel,
        grid=(t // bm,),
        in_specs=[pl.BlockSpec((bm, d), lambda i: (i, 0)), pl.BlockSpec((1, d), lambda i: (0, 0))],
        out_specs=pl.BlockSpec((bm, d), lambda i: (i, 0)),
        out_shape=jax.ShapeDtypeStruct((t, d), F32),
        compiler_params=_params("arbitrary"),
        name="final_norm",
    )(x, g.reshape(1, d))


def _rope_tables(n):
    pairs = DH // 4
    rows = n // GRID_W
    t_row = jnp.repeat(jnp.arange(rows), GRID_W).astype(F32)
    t_col = jnp.tile(jnp.arange(GRID_W), rows).astype(F32)
    inv = jnp.power(ROPE_BASE, -jnp.arange(pairs, dtype=F32) / pairs)
    ang_row = t_row[:, None] * inv
    ang_col = t_col[:, None] * inv
    ang = jnp.concatenate([ang_row, ang_row, ang_col, ang_col], axis=1)
    sign = np.tile(np.repeat(np.array([-1.0, 1.0], np.float32), pairs), 2)
    cos = jnp.cos(ang)
    sin = jnp.sin(ang) * sign
    reps = LANES // DH
    return jnp.tile(cos, (1, reps)), jnp.tile(sin, (1, reps))


def _block_diag_gates(w_rg_a, b_rg_a, w_rg_x, b_rg_x, lru_lambda, dm):
    depth = w_rg_a.shape[0]
    cw = LANES
    nck = dm // cw
    per = cw // BS_R

    def dense(w):
        w = w.reshape(depth, 2, nck, per, BS_R, BS_R)
        eye = jnp.eye(per, dtype=w.dtype)
        full = jnp.einsum('ldcpij,pq->ldcpiqj', w, eye)
        return full.reshape(depth, 2, nck, cw, cw)

    wa, wx = dense(w_rg_a), dense(w_rg_x)
    wg = jnp.concatenate([wa[:, 0], wx[:, 0], wa[:, 1], wx[:, 1]], axis=-1).astype(BF16)

    def vec(v):
        return v.reshape(depth, 2, nck, 1, cw)

    ba, bx = vec(b_rg_a), vec(b_rg_x)
    bg = jnp.concatenate([ba[:, 0], bx[:, 0], ba[:, 1], bx[:, 1]], axis=-1)
    lam = vec(lru_lambda)
    lam = jnp.concatenate([lam[:, 0], lam[:, 1]], axis=-1)
    return wg, bg, lam


def kernel(x_prompt, x_sample, cache_k, cache_v, state_lru, c, c_ctx, w_mod, b_mod, g_norm1, g_norm2, g_final, w_in, lam_q1, lam_k1, lam_q2, lam_k2, g_subln, w_dw31, b_dw31, g_ln_conv, b_ln_conv, w_dw3, w_conv4, b_conv4, w_rg_a, b_rg_a, w_rg_x, b_rg_x, lru_lambda, w_branch, w_out, w_ffn_up, w_ffn_conv, b_ffn_conv, w_ffn_down):
    depth, d, _ = w_in.shape
    dm = d // 4
    bc, sc, _ = x_prompt.shape
    bd, sd, _ = x_sample.shape
    past = cache_k.shape[2]
    d_ff = w_ffn_conv.shape[-1]
    ffp = -(-d_ff // 512) * 512

    rows = -(-(bd + 1) // SUBLANES) * SUBLANES
    cvec = jnp.zeros((rows, d), F32).at[:bd].set(c).at[bd].set(c_ctx)
    mod = _modulation(cvec, w_mod, b_mod).reshape(depth, rows, 6, 1, d)

    qkvz = jnp.concatenate([w_in[:, :, :5 * dm], w_in[:, :, 6 * dm:8 * dm], w_in[:, :, 5 * dm:6 * dm],
                            w_in[:, :, 8 * dm:10 * dm]], axis=-1).astype(BF16)
    w_gate = w_in[:, :, 10 * dm:].astype(BF16)
    w_branch_b = w_branch.astype(BF16)
    w_out_b = w_out.astype(BF16)
    padc = [(0, 0), (0, 0), (0, ffp - d_ff)]
    w_up_a = jnp.pad(w_ffn_up[:, :, :d_ff], padc).astype(BF16)
    w_up_u = jnp.pad(w_ffn_up[:, :, d_ff:], padc).astype(BF16)
    w_fc = jnp.pad(w_ffn_conv, padc)
    b_fc = jnp.pad(b_ffn_conv, [(0, 0), (0, ffp - d_ff)]).reshape(depth, 1, ffp)
    w_down = jnp.pad(w_ffn_down, [(0, 0), (0, ffp - d_ff), (0, 0)]).astype(BF16)
    wg_bd, bg_bd, lam_bd = _block_diag_gates(w_rg_a, b_rg_a, w_rg_x, b_rg_x, lru_lambda, dm)

    def v3(p):
        return p.reshape(depth, 1, p.shape[-1])

    g1, g2 = v3(g_norm1), v3(g_norm2)
    lam_p = [v3(lam_q1), v3(lam_k1), v3(lam_q2), v3(lam_k2)]
    g_sub = v3(g_subln)
    b31, gln, bln, b4 = v3(b_dw31), v3(g_ln_conv), v3(b_ln_conv), v3(b_conv4)

    def layer_pass(x, layer, b, n, rpm, base, tables, ctx_k, ctx_v, h0, want_kv):
        outs = _in_projection(x, mod, layer, rpm, base, g1, qkvz, tables, n, want_kv)
        if want_kv:
            q, k, v, kv32, z = outs
        else:
            q, k, v, z = outs
            kv32 = None
        q, k, v, z = (a.reshape(b, n, -1) for a in (q, k, v, z))
        y_a = _attention(q, k, v, ctx_k, ctx_v, layer, lam_p, g_sub)
        y_b = _conv_module(z, layer, w_dw31, b31, gln, bln)
        y_c = _short_conv(z, layer, w_dw3)
        y_d, h_last = _rglru(z, h0, layer, w_conv4, b4, wg_bd, bg_bd, lam_bd)
        ys = [y.reshape(b * n, dm) for y in (y_a, y_b, y_c, y_d)]
        merged = _merge(x, mod, layer, rpm, base, g1, ys, w_gate, w_branch_b)
        x1 = _residual_proj(merged, w_out_b, x, mod, layer, 2, rpm, base, "out_projection",
                            row_pref=512, col_pref=d)
        act = _ffn_up(x1, mod, layer, rpm, base, g2, w_up_a, w_up_u, w_fc, b_fc, n)
        x2 = _residual_proj(act, w_down, x1, mod, layer, 5, rpm, base, "ffn_down")
        return x2, kv32, h_last

    xp = x_prompt.reshape(bc * sc, d)
    zero_state = jnp.zeros((bc, 2, dm), F32)
    ks, vs, ss = [], [], []
    for layer in range(depth):
        xp, kv32, h_last = layer_pass(xp, layer, bc, sc, bc * sc, bd, None, None, None, zero_state, True)
        kv32 = kv32.reshape(bc, sc, 2 * dm)
        ks.append(kv32[:, :, :dm])
        vs.append(kv32[:, :, dm:])
        ss.append(h_last)
    heads = dm // V_DIM
    y_prompt = _final_norm(xp, g_final).reshape(bc, sc, d)
    new_cache_k = jnp.stack(ks, axis=1).reshape(bc, depth, sc, heads, 2, DH)
    new_cache_v = jnp.stack(vs, axis=1).reshape(bc, depth, sc, heads, V_DIM)
    new_state = jnp.stack(ss, axis=1)

    tables = _rope_tables(sd)
    ck = cache_k.reshape(bd, depth, past, dm).astype(BF16)
    cv = _values_t(cache_v.reshape(bd, depth, past, dm).astype(BF16), dm // V_DIM)
    xs = x_sample.reshape(bd * sd, d)
    for layer in range(depth):
        xs, _, _ = layer_pass(xs, layer, bd, sd, sd, 0, tables, ck, cv, state_lru[:, layer], False)
    y_sample = _final_norm(xs, g_final).reshape(bd, sd, d)
    return (y_prompt, y_sample, new_cache_k, new_cache_v, new_state)
```
